```python
import math
import jax, jax.numpy as jnp
from jax import lax
import numpy as np

D_MODEL = 4096
BATCH = 4
SEQ = 2048
DEPTH = 4
DEC_BATCH = 128
DEC_SEQ = 8
PAST_LEN = 8192
PAGE_SIZE = 128

N_EVEN = (DEPTH + 1) // 2
N_ODD = DEPTH // 2
D_FF = 4 * D_MODEL
ROPE_THETA = 500000.0
Q_BLOCK = 128
EPS = 1e-6
ADA_CHUNKS = 6

A_HEADS = D_MODEL // 256
A_DIM = 64
A_ROT = A_DIM // 4
B_HEADS = D_MODEL // 256
B_NOPE = 128
B_ROPE = 64
B_V = 128
B_Q_LORA = D_MODEL // 4
B_KV_LORA = D_MODEL // 16
C_HEADS = D_MODEL // 256
C_DIM = 128
C_ROT = C_DIM // 4
MOBA_BLOCK = 256
MOBA_TOPK = 3
D_HEADS = D_MODEL // 256
D_DIM = 128

EVEN_SPLITS = (A_HEADS * 2 * A_DIM, 2 * A_DIM, 2 * A_DIM, B_Q_LORA, B_KV_LORA, B_ROPE)
ODD_SPLITS = (C_HEADS * C_DIM, C_DIM, C_DIM, D_HEADS * D_DIM, D_DIM, D_DIM)
EVEN_CUTS = tuple(int(v) for v in np.cumsum(EVEN_SPLITS)[:-1])
ODD_CUTS = tuple(int(v) for v in np.cumsum(ODD_SPLITS)[:-1])
EVEN_IN = int(sum(EVEN_SPLITS))
ODD_IN = int(sum(ODD_SPLITS))
EVEN_OUT = A_HEADS * 2 * A_DIM + B_HEADS * B_V
ODD_OUT = C_HEADS * C_DIM + D_HEADS * D_DIM
EVEN_CACHE = ('a_k', 'a_v', 'b_ckv', 'b_kr')
ODD_CACHE = ('c_k', 'c_v', 'd_k', 'd_v')

kernel_name = 'hybrid_diffmla_mobasb_decoder_step'


def rmsnorm(x, gain=None):
    xf = x.astype(jnp.float32)
    y = xf * lax.rsqrt(jnp.mean(xf * xf, axis=-1, keepdims=True) + EPS)
    if gain is not None:
        y = y * gain.astype(jnp.float32)
    return y.astype(x.dtype)


def apply_rope(x, pos, rot_dim):
    half = rot_dim // 2
    inv = ROPE_THETA ** (-jnp.arange(half, dtype=jnp.float32) * 2.0 / rot_dim)
    ang = pos.astype(jnp.float32)[:, None] * inv[None, :]
    shape = (1, pos.shape[0]) + (1,) * (x.ndim - 3) + (half,)
    cos = jnp.cos(ang).reshape(shape)
    sin = jnp.sin(ang).reshape(shape)
    xf = x.astype(jnp.float32)
    x1 = xf[..., :half]
    x2 = xf[..., half:rot_dim]
    out = jnp.concatenate([x1 * cos - x2 * sin, x2 * cos + x1 * sin, xf[..., rot_dim:]], axis=-1)
    return out.astype(x.dtype)


def sweep_queries(fn, q_args, q_pos):
    T = q_pos.shape[0]
    qb = min(Q_BLOCK, T)
    nc = T // qb
    chunks = tuple(jnp.moveaxis(a.reshape(a.shape[0], nc, qb, *a.shape[2:]), 1, 0) for a in q_args)
    out = lax.map(lambda args: fn(args[:-1], args[-1]), chunks + (q_pos.reshape(nc, qb),))
    out = jnp.moveaxis(out, 0, 1)
    return out.reshape(out.shape[0], T, *out.shape[3:])


def causal_mask(q_pos, n_keys, strict=False):
    k_pos = jnp.arange(n_keys)
    if strict:
        return k_pos[None, :] < q_pos[:, None]
    return k_pos[None, :] <= q_pos[:, None]


def diff_attention(q1, q2, k1, k2, v, q_pos, lam):
    scale = A_DIM ** -0.5
    n_keys = k1.shape[1]

    def block(qs, qp):
        b1, b2 = qs
        mask = causal_mask(qp, n_keys)

        def probs(q, k):
            s = jnp.einsum('bqhd,bkd->bhqk', q, k).astype(jnp.float32) * scale
            return jax.nn.softmax(jnp.where(mask, s, -jnp.inf), axis=-1)

        w = probs(b1, k1) - lam * probs(b2, k2)
        return jnp.einsum('bhqk,bkd->bqhd', w.astype(v.dtype), v)

    return sweep_queries(block, (q1, q2), q_pos)


def latent_attention(q_lat, q_rope, ckv, kr, q_pos):
    scale = (B_NOPE + B_ROPE) ** -0.5
    n_keys = ckv.shape[1]

    def block(qs, qp):
        ql, qr = qs
        s = jnp.einsum('bqhc,bkc->bhqk', ql, ckv) + jnp.einsum('bqhr,bkr->bhqk', qr, kr)
        s = jnp.where(causal_mask(qp, n_keys), s.astype(jnp.float32) * scale, -jnp.inf)
        p = jax.nn.softmax(s, axis=-1)
        return jnp.einsum('bhqk,bkc->bqhc', p.astype(ckv.dtype), ckv)

    return sweep_queries(block, (q_lat, q_rope), q_pos)


def moba_attention(q, k, v, q_pos):
    B, L, dk = k.shape
    nb = -(-L // MOBA_BLOCK)
    pad = nb * MOBA_BLOCK - L
    kb = jnp.pad(k, ((0, 0), (0, pad), (0, 0))).reshape(B, nb, MOBA_BLOCK, dk)
    vb = jnp.pad(v, ((0, 0), (0, pad), (0, 0))).reshape(B, nb, MOBA_BLOCK, dk)
    kmean = jnp.mean(kb.astype(jnp.float32), axis=2)
    n_sel = min(MOBA_TOPK, nb)
    scale = C_DIM ** -0.5
    blk_ids = jnp.arange(nb)
    offs = jnp.arange(MOBA_BLOCK)

    def block(qs, qp):
        (qc,) = qs
        qb = qp.shape[0]
        H = qc.shape[2]
        own = qp // MOBA_BLOCK
        g = jnp.einsum('bqhd,bnd->bqhn', qc.astype(jnp.float32), kmean)
        past = blk_ids[None, :] < own[:, None]
        g = jnp.where(past[None, :, None, :], g, -jnp.inf)
        _, sel = lax.top_k(g, n_sel)
        sel_ok = sel < own[None, :, None, None]
        own_b = jnp.broadcast_to(own[None, :, None, None], (B, qb, H, 1))
        blocks = jnp.concatenate([sel, own_b], axis=-1)
        ok = jnp.concatenate([sel_ok, jnp.ones((B, qb, H, 1), bool)], axis=-1)
        key_pos = blocks[..., None] * MOBA_BLOCK + offs
        valid = ok[..., None] & (key_pos <= qp[None, :, None, None, None])

        def head(args):
            qh, bh, vh = args
            kg = jax.vmap(lambda kb_b, i: kb_b[i])(kb, bh)
            vg = jax.vmap(lambda vb_b, i: vb_b[i])(vb, bh)
            s = jnp.einsum('bqd,bqsjd->bqsj', qh, kg).astype(jnp.float32) * scale
            s = jnp.where(vh, s, -jnp.inf)
            p = jax.nn.softmax(s.reshape(B, qb, -1), axis=-1).reshape(s.shape)
            return jnp.einsum('bqsj,bqsjd->bqd', p.astype(vg.dtype), vg)

        o = lax.map(head, (jnp.moveaxis(qc, 2, 0), jnp.moveaxis(blocks, 2, 0), jnp.moveaxis(valid, 2, 0)))
        return jnp.moveaxis(o, 0, 2)

    return sweep_queries(block, (q,), q_pos)


def stickbreak_attention(q, k, v, q_pos):
    scale = D_DIM ** -0.5
    n_keys = k.shape[1]

    def block(qs, qp):
        (qc,) = qs
        z = jnp.einsum('bqhd,bkd->bhqk', qc, k).astype(jnp.float32) * scale
        strict = causal_mask(qp, n_keys, strict=True)
        log_beta = jax.nn.log_sigmoid(z)
        log_1m = jnp.where(strict, jax.nn.log_sigmoid(-z), 0.0)
        tail = lax.cumsum(log_1m, axis=3, reverse=True) - log_1m
        a = jnp.where(strict, jnp.exp(log_beta + tail), 0.0)
        return jnp.einsum('bhqk,bkd->bqhd', a.astype(v.dtype), v)

    return sweep_queries(block, (q,), q_pos)


def sq_relu_mlp(h, w_up, w_down):
    u = jax.nn.relu(h @ w_up)
    return (u * u) @ w_down


def gather_pages(pool, page_table):
    g = pool[page_table]
    return g.reshape(page_table.shape[0], -1, pool.shape[-1])


def with_past(new, old):
    return new if old is None else jnp.concatenate([old, new], axis=1)


def even_mixer(h, pos, hist, W, i, layer):
    B, T, _ = h.shape
    qa, ka, va, qlb, ckv, kr = jnp.split(h @ W['w_in_even'][i], EVEN_CUTS, axis=-1)
    qa = apply_rope(rmsnorm(qa.reshape(B, T, A_HEADS, 2, A_DIM), W['a_q_norm'][i]), pos, A_ROT)
    ka = apply_rope(rmsnorm(ka.reshape(B, T, 2, A_DIM), W['a_k_norm'][i]), pos, A_ROT)
    a_k_new = ka.reshape(B, T, 2 * A_DIM)
    a_v_new = va
    ckv = rmsnorm(ckv, W['b_kv_norm'][i])
    kr = apply_rope(rmsnorm(kr, W['b_kr_norm'][i]), pos, B_ROPE)
    qh = jnp.einsum('btc,chd->bthd', rmsnorm(qlb, W['b_q_lat_norm'][i]), W['b_w_q_up'][i])
    qh = rmsnorm(qh, W['b_q_norm'][i])
    q_lat = jnp.einsum('bthn,chn->bthc', qh[..., :B_NOPE], W['b_w_uk'][i])
    q_rope = apply_rope(qh[..., B_NOPE:], pos, B_ROPE)
    h_ak, h_av, h_ckv, h_kr = (None, None, None, None) if hist is None else hist
    ak_all = with_past(a_k_new, h_ak)
    av_all = with_past(a_v_new, h_av)
    ckv_all = with_past(ckv, h_ckv)
    kr_all = with_past(kr, h_kr)
    lam_init = 0.8 - 0.6 * math.exp(-0.3 * layer)
    f32 = jnp.float32
    lam = (jnp.exp(jnp.sum(W['a_lambda_q1'][i].astype(f32) * W['a_lambda_k1'][i].astype(f32)))
           - jnp.exp(jnp.sum(W['a_lambda_q2'][i].astype(f32) * W['a_lambda_k2'][i].astype(f32))) + lam_init)
    L = ak_all.shape[1]
    ak_all = ak_all.reshape(B, L, 2, A_DIM)
    o_a = diff_attention(qa[..., 0, :], qa[..., 1, :], ak_all[..., 0, :], ak_all[..., 1, :], av_all, pos, lam)
    o_a = rmsnorm(o_a, W['a_sub_norm'][i]) * (1.0 - lam_init)
    o_lat = latent_attention(q_lat, q_rope, ckv_all, kr_all, pos)
    o_b = jnp.einsum('bthc,chv->bthv', o_lat, W['b_w_uv'][i])
    o = jnp.concatenate([o_a.reshape(B, T, -1), o_b.reshape(B, T, -1)], axis=-1) @ W['w_out_even'][i]
    return o, (a_k_new, a_v_new, ckv, kr)


def odd_mixer(h, pos, hist, W, i):
    B, T, _ = h.shape
    qc, kc, vc, qd, kd, vd = jnp.split(h @ W['w_in_odd'][i], ODD_CUTS, axis=-1)
    qc = apply_rope(rmsnorm(qc.reshape(B, T, C_HEADS, C_DIM), W['c_q_norm'][i]), pos, C_ROT)
    kc = apply_rope(rmsnorm(kc, W['c_k_norm'][i]), pos, C_ROT)
    qd = qd.reshape(B, T, D_HEADS, D_DIM)
    h_ck, h_cv, h_dk, h_dv = (None, None, None, None) if hist is None else hist
    o_c = moba_attention(qc, with_past(kc, h_ck), with_past(vc, h_cv), pos)
    o_d = stickbreak_attention(qd, with_past(kd, h_dk), with_past(vd, h_dv), pos)
    o = jnp.concatenate([o_c.reshape(B, T, -1), o_d.reshape(B, T, -1)], axis=-1) @ W['w_out_odd'][i]
    return o, (kc, vc, kd, vd)


def trunk(x, c, pos, past, page_table, W):
    B, _, D = x.shape
    base = (jax.nn.silu(c) @ W['w_ada'] + W['b_ada']).reshape(B, ADA_CHUNKS, D)
    rows = {n: [] for n in EVEN_CACHE + ODD_CACHE}
    for l in range(DEPTH):
        i = l // 2
        mod = base + W['ada_table'][l]
        shift1, scale1, gate1, shift2, scale2, gate2 = [mod[:, j, None, :] for j in range(ADA_CHUNKS)]
        h = rmsnorm(x) * (1 + scale1) + shift1
        if l % 2 == 0:
            hist = None if past is None else tuple(gather_pages(past[n][i], page_table) for n in EVEN_CACHE)
            o, new = even_mixer(h, pos, hist, W, i, l)
            names = EVEN_CACHE
        else:
            hist = None if past is None else tuple(gather_pages(past[n][i], page_table) for n in ODD_CACHE)
            o, new = odd_mixer(h, pos, hist, W, i)
            names = ODD_CACHE
        for n, r in zip(names, new):
            rows[n].append(r)
        x = x + gate1 * o
        h = rmsnorm(x) * (1 + scale2) + shift2
        x = x + gate2 * sq_relu_mlp(h, W['w_mlp_up'][l], W['w_mlp_down'][l])
    return x, {n: jnp.stack(v) for n, v in rows.items()}


def setup_inputs(seed: int = 0) -> dict:
    key = jax.random.key(seed)
    ks = iter(jax.random.split(key, 48))
    f32 = jnp.float32

    def nrm(shape, scale=1.0):
        return jax.random.normal(next(ks), shape, f32) * scale

    def gain(shape):
        return 1.0 + nrm(shape, 0.02)

    n_pages = PAST_LEN // PAGE_SIZE
    n_used = DEC_BATCH * n_pages
    n_phys = n_used + n_used // 4

    def pool(n_layers, width):
        return nrm((n_layers, n_phys, PAGE_SIZE, width))

    inputs = {}
    inputs['x_prompt'] = nrm((BATCH, SEQ, D_MODEL))
    inputs['x_sample'] = nrm((DEC_BATCH, DEC_SEQ, D_MODEL))
    inputs['cache_a_k'] = pool(N_EVEN, 2 * A_DIM)
    inputs['cache_a_v'] = pool(N_EVEN, 2 * A_DIM)
    inputs['cache_b_ckv'] = pool(N_EVEN, B_KV_LORA)
    inputs['cache_b_kr'] = pool(N_EVEN, B_ROPE)
    inputs['cache_c_k'] = pool(N_ODD, C_DIM)
    inputs['cache_c_v'] = pool(N_ODD, C_DIM)
    inputs['cache_d_k'] = pool(N_ODD, D_DIM)
    inputs['cache_d_v'] = pool(N_ODD, D_DIM)
    perm = jax.random.permutation(next(ks), n_phys)[:n_used]
    inputs['page_table'] = perm.reshape(DEC_BATCH, n_pages).astype(jnp.int32)
    inputs['c_prompt'] = nrm((BATCH, D_MODEL))
    inputs['c_sample'] = nrm((DEC_BATCH, D_MODEL))
    inputs['w_ada'] = nrm((D_MODEL, ADA_CHUNKS * D_MODEL), 0.3 * D_MODEL ** -0.5)
    inputs['b_ada'] = nrm((ADA_CHUNKS * D_MODEL,), 0.02)
    inputs['ada_table'] = nrm((DEPTH, ADA_CHUNKS, D_MODEL), 0.1)
    inputs['w_in_even'] = nrm((N_EVEN, D_MODEL, EVEN_IN), D_MODEL ** -0.5)
    inputs['a_q_norm'] = gain((N_EVEN, A_DIM))
    inputs['a_k_norm'] = gain((N_EVEN, A_DIM))
    inputs['a_lambda_q1'] = nrm((N_EVEN, A_DIM), 0.1)
    inputs['a_lambda_k1'] = nrm((N_EVEN, A_DIM), 0.1)
    inputs['a_lambda_q2'] = nrm((N_EVEN, A_DIM), 0.1)
    inputs['a_lambda_k2'] = nrm((N_EVEN, A_DIM), 0.1)
    inputs['a_sub_norm'] = gain((N_EVEN, 2 * A_DIM))
    inputs['b_q_lat_norm'] = gain((N_EVEN, B_Q_LORA))
    inputs['b_w_q_up'] = nrm((N_EVEN, B_Q_LORA, B_HEADS, B_NOPE + B_ROPE), B_Q_LORA ** -0.5)
    inputs['b_q_norm'] = gain((N_EVEN, B_NOPE + B_ROPE))
    inputs['b_kv_norm'] = gain((N_EVEN, B_KV_LORA))
    inputs['b_kr_norm'] = gain((N_EVEN, B_ROPE))
    inputs['b_w_uk'] = nrm((N_EVEN, B_KV_LORA, B_HEADS, B_NOPE), B_KV_LORA ** -0.5)
    inputs['b_w_uv'] = nrm((N_EVEN, B_KV_LORA, B_HEADS, B_V), B_KV_LORA ** -0.5)
    inputs['w_out_even'] = nrm((N_EVEN, EVEN_OUT, D_MODEL), EVEN_OUT ** -0.5)
    inputs['w_in_odd'] = nrm((N_ODD, D_MODEL, ODD_IN), D_MODEL ** -0.5)
    inputs['c_q_norm'] = gain((N_ODD, C_DIM))
    inputs['c_k_norm'] = gain((N_ODD, C_DIM))
    inputs['w_out_odd'] = nrm((N_ODD, ODD_OUT, D_MODEL), ODD_OUT ** -0.5)
    inputs['w_mlp_up'] = nrm((DEPTH, D_MODEL, D_FF), D_MODEL ** -0.5)
    inputs['w_mlp_down'] = nrm((DEPTH, D_FF, D_MODEL), D_FF ** -0.5)
    return inputs


def reference(x_prompt, x_sample, cache_a_k, cache_a_v, cache_b_ckv, cache_b_kr, cache_c_k, cache_c_v,
              cache_d_k, cache_d_v, page_table, c_prompt, c_sample, w_ada, b_ada, ada_table,
              w_in_even, a_q_norm, a_k_norm, a_lambda_q1, a_lambda_k1, a_lambda_q2, a_lambda_k2, a_sub_norm,
              b_q_lat_norm, b_w_q_up, b_q_norm, b_kv_norm, b_kr_norm, b_w_uk, b_w_uv, w_out_even,
              w_in_odd, c_q_norm, c_k_norm, w_out_odd, w_mlp_up, w_mlp_down):
    W = dict(w_ada=w_ada, b_ada=b_ada, ada_table=ada_table, w_in_even=w_in_even, a_q_norm=a_q_norm,
             a_k_norm=a_k_norm, a_lambda_q1=a_lambda_q1, a_lambda_k1=a_lambda_k1, a_lambda_q2=a_lambda_q2,
             a_lambda_k2=a_lambda_k2, a_sub_norm=a_sub_norm, b_q_lat_norm=b_q_lat_norm, b_w_q_up=b_w_q_up,
             b_q_norm=b_q_norm, b_kv_norm=b_kv_norm, b_kr_norm=b_kr_norm, b_w_uk=b_w_uk, b_w_uv=b_w_uv,
             w_out_even=w_out_even, w_in_odd=w_in_odd, c_q_norm=c_q_norm, c_k_norm=c_k_norm,
             w_out_odd=w_out_odd, w_mlp_up=w_mlp_up, w_mlp_down=w_mlp_down)
    pools = dict(a_k=cache_a_k, a_v=cache_a_v, b_ckv=cache_b_ckv, b_kr=cache_b_kr,
                 c_k=cache_c_k, c_v=cache_c_v, d_k=cache_d_k, d_v=cache_d_v)
    past_len = page_table.shape[1] * PAGE_SIZE
    pos_p = jnp.arange(x_prompt.shape[1])
    pos_s = past_len + jnp.arange(x_sample.shape[1])
    y_prompt, rp = trunk(x_prompt, c_prompt, pos_p, None, None, W)
    y_sample, rs = trunk(x_sample, c_sample, pos_s, pools, page_table, W)
    return (y_prompt, y_sample,
            rp['a_k'], rs['a_k'], rp['a_v'], rs['a_v'],
            rp['b_ckv'], rs['b_ckv'], rp['b_kr'], rs['b_kr'],
            rp['c_k'], rs['c_k'], rp['c_v'], rs['c_v'],
            rp['d_k'], rs['d_k'], rp['d_v'], rs['d_v'])
```

```python
import functools
import math

import numpy as np
import jax
import jax.numpy as jnp
from jax import lax
from jax.experimental import pallas as pl
from jax.experimental.pallas import tpu as pltpu

F32 = jnp.float32
BF16 = jnp.bfloat16

EPS = 1e-6
ROPE_THETA = 500000.0
ADA_CHUNKS = 6
PAGE_SIZE = 128
MASKED_SCORE = -1e30
ALL_KEYS = 2 ** 30

LANES = 128
HEADS = 16
A_DIM = 64
A_ROT = 16
B_NOPE = 128
B_ROPE = 64
B_QK = B_NOPE + B_ROPE
B_HEAD_SLAB = 256
B_Q_LORA = 1024
B_KV_LORA = 256
B_Q_WIDTH = B_KV_LORA + LANES
C_DIM = 128
C_ROT = 32
D_DIM = 128
MOBA_BLOCK = 256
MOBA_TOPK = 3
MOBA_SHIFT = 8
KEY_BLOCK = 256
VMEM_LIMIT_BYTES = 56 * 1024 * 1024

EVEN_QA, EVEN_QLB, EVEN_KA, EVEN_VA, EVEN_CKV, EVEN_KR, EVEN_WIDTH = 0, 2048, 3072, 3200, 3328, 3584, 3840
ODD_QC, ODD_QD, ODD_KC, ODD_VC, ODD_KD, ODD_VD, ODD_WIDTH = 0, 2048, 4096, 4224, 4352, 4480, 4608


def _cparams(sem):
    return pltpu.CompilerParams(dimension_semantics=sem, vmem_limit_bytes=VMEM_LIMIT_BYTES)


def _dot(a, b):
    return jnp.dot(a, b, preferred_element_type=F32)


def _dot_nt(a, b):
    return lax.dot_general(a, b, (((1,), (1,)), ((), ())), preferred_element_type=F32)


def _split_bf16(x):
    hi = x.astype(BF16)
    lo = (x - hi.astype(F32)).astype(BF16)
    return hi, lo


def _mm_kernel(*refs, nk, silu, has_bias, relu2, has_res):
    it = iter(refs)
    x_ref = next(it)
    w_ref = next(it)
    b_ref = next(it) if has_bias else None
    r_ref = next(it) if has_res else None
    g_ref = next(it) if has_res else None
    o_ref = next(it)
    acc_ref = next(it) if nk > 1 else None

    x = x_ref[...]
    if silu:
        x = (x * (1.0 / (1.0 + jnp.exp(-x)))).astype(BF16)
    part = _dot(x, w_ref[...])

    def epilogue(acc):
        if has_bias:
            acc = acc + b_ref[...]
        if relu2:
            acc = jnp.maximum(acc, 0.0)
            acc = acc * acc
        if has_res:
            acc = r_ref[...] + g_ref[...] * acc
        o_ref[...] = acc.astype(o_ref.dtype)

    if nk == 1:
        epilogue(part)
    else:
        k = pl.program_id(2)

        @pl.when(k == 0)
        def _():
            acc_ref[...] = part

        @pl.when(k > 0)
        def _():
            acc_ref[...] += part

        @pl.when(k == nk - 1)
        def _():
            epilogue(acc_ref[...])


def _pick_tile(n, pref, unit=LANES):
    if n <= pref:
        return n
    t = pref - pref % unit
    while n % t:
        t -= unit
    return t


def _matmul(x, w, layer, *, out_dtype, tm=1024, tn=1024, tk=4096, silu=False, bias=None, relu2=False,
            res=None, gate=None, gate_spec=None):
    M, K = x.shape
    N = w.shape[-1]
    tm, tn, tk = _pick_tile(M, tm), _pick_tile(N, tn), _pick_tile(K, tk)
    nk = K // tk
    in_specs = [pl.BlockSpec((tm, tk), lambda i, j, k: (i, k)),
                pl.BlockSpec((None, tk, tn), lambda i, j, k: (layer, k, j))]
    args = [x, w]
    if bias is not None:
        in_specs.append(pl.BlockSpec((1, tn), lambda i, j, k: (0, j)))
        args.append(bias)
    if res is not None:
        in_specs.append(pl.BlockSpec((tm, tn), lambda i, j, k: (i, j)))
        in_specs.append(gate_spec(tm, tn))
        args += [res, gate]
    scratch = [pltpu.VMEM((tm, tn), F32)] if nk > 1 else []
    kern = functools.partial(_mm_kernel, nk=nk, silu=silu, has_bias=bias is not None, relu2=relu2,
                             has_res=res is not None)
    return pl.pallas_call(
        kern, grid=(M // tm, N // tn, nk), in_specs=in_specs,
        out_specs=pl.BlockSpec((tm, tn), lambda i, j, k: (i, j)),
        out_shape=jax.ShapeDtypeStruct((M, N), out_dtype), scratch_shapes=scratch,
        compiler_params=_cparams(("parallel", "parallel", "arbitrary")), name="matmul")(*args)


def _prenorm_kernel(x_ref, shift_ref, scale_ref, o_ref):
    x = x_ref[...]
    y = x * lax.rsqrt(jnp.mean(x * x, axis=-1, keepdims=True) + EPS)
    o_ref[...] = (y * (1.0 + scale_ref[...]) + shift_ref[...]).astype(o_ref.dtype)


def _prenorm(x, mod, shift_spec, scale_spec, tm=256):
    M, D = x.shape
    tm = _pick_tile(M, tm)
    return pl.pallas_call(
        _prenorm_kernel, grid=(M // tm,),
        in_specs=[pl.BlockSpec((tm, D), lambda i: (i, 0)), shift_spec(tm), scale_spec(tm)],
        out_specs=pl.BlockSpec((tm, D), lambda i: (i, 0)),
        out_shape=jax.ShapeDtypeStruct((M, D), BF16),
        compiler_params=_cparams(("parallel",)), name="prenorm")(x, mod, mod)


def _lane_iota(shape):
    return lax.broadcasted_iota(jnp.int32, shape, len(shape) - 1)


def _rope(y, tab, half):
    c, s_up, s_dn = tab
    return y * c + pltpu.roll(y, LANES - half, 1) * s_up + pltpu.roll(y, half, 1) * s_dn


def _norm_pair64(x, gain):
    lo = _lane_iota(x.shape) < A_DIM
    sq = x * x
    s_lo = jnp.sum(jnp.where(lo, sq, 0.0), axis=-1, keepdims=True)
    s_hi = jnp.sum(jnp.where(lo, 0.0, sq), axis=-1, keepdims=True)
    inv = jnp.where(lo, lax.rsqrt(s_lo * (1.0 / A_DIM) + EPS), lax.rsqrt(s_hi * (1.0 / A_DIM) + EPS))
    return x * inv * gain


def _norm_rows(x, gain, n_real):
    ms = jnp.sum(x * x, axis=-1, keepdims=True) * (1.0 / n_real)
    return x * lax.rsqrt(ms + EPS) * gain


def _rope_tables(pos, rot, width, period):
    half = rot // 2
    inv = ROPE_THETA ** (-jnp.arange(half, dtype=F32) * 2.0 / rot)
    ang = pos.astype(F32)[:, None] * inv[None, :]
    cos, sin = jnp.cos(ang), jnp.sin(ang)
    lane = np.arange(LANES)
    g = lane % period
    first = (g < half) & (lane < width)
    second = (g >= half) & (g < rot) & (lane < width)
    idx = np.where(first, g, np.where(second, g - half, 0))
    cos_l, sin_l = cos[:, idx], sin[:, idx]
    rot_l = jnp.asarray(first | second)
    c = jnp.where(rot_l[None, :], cos_l, 1.0)
    s_up = jnp.where(jnp.asarray(first)[None, :], -sin_l, 0.0)
    s_dn = jnp.where(jnp.asarray(second)[None, :], sin_l, 0.0)
    return c, s_up, s_dn


def _prep_even_kernel(qlb_ref, ka_ref, ckv_ref, kr_ref, ca, sua, sda, cr, sur, sdr,
                      g_ka, g_qlb, g_ckv, g_kr, qlbn_o, ka_o, ckv_o, kr_o):
    qlb = qlb_ref[...]
    qlbn_o[...] = _norm_rows(qlb, g_qlb[...], B_Q_LORA).astype(qlbn_o.dtype)
    ka = _norm_pair64(ka_ref[...], g_ka[...])
    ka_o[...] = _rope(ka, (ca[...], sua[...], sda[...]), A_ROT // 2)
    ckv_o[...] = _norm_rows(ckv_ref[...], g_ckv[...], B_KV_LORA)
    kr = _norm_rows(kr_ref[...], g_kr[...], B_ROPE)
    kr = _rope(kr, (cr[...], sur[...], sdr[...]), B_ROPE // 2)
    kr_o[...] = kr[:, :B_ROPE]


def _prep_odd_kernel(kc_ref, cc, suc, sdc, g_kc, kc_o):
    kc = _norm_rows(kc_ref[...], g_kc[...], C_DIM)
    kc_o[...] = _rope(kc, (cc[...], suc[...], sdc[...]), C_ROT // 2)


def _table_specs(tm, n_tab_blocks):
    return [pl.BlockSpec((tm, LANES), lambda i: (i % n_tab_blocks, 0))] * 3


def _full(shape):
    return pl.BlockSpec(shape, lambda *_: (0,) * len(shape))


def _prep_even(z, tab_a, tab_r, g_ka, g_qlb, g_ckv, g_kr, tm):
    M = z.shape[0]
    nt = tab_a[0].shape[0] // tm
    col = lambda width, start: pl.BlockSpec((tm, width), lambda i: (i, start // width))
    row = lambda width: pl.BlockSpec((tm, width), lambda i: (i, 0))
    return pl.pallas_call(
        _prep_even_kernel, grid=(M // tm,),
        in_specs=[col(B_Q_LORA, EVEN_QLB), col(LANES, EVEN_KA), col(B_KV_LORA, EVEN_CKV), col(LANES, EVEN_KR)]
        + _table_specs(tm, nt) + _table_specs(tm, nt)
        + [_full((1, LANES)), _full((1, B_Q_LORA)), _full((1, B_KV_LORA)), _full((1, LANES))],
        out_specs=[row(B_Q_LORA), row(LANES), row(B_KV_LORA), row(B_ROPE)],
        out_shape=[jax.ShapeDtypeStruct((M, B_Q_LORA), BF16), jax.ShapeDtypeStruct((M, LANES), F32),
                   jax.ShapeDtypeStruct((M, B_KV_LORA), F32), jax.ShapeDtypeStruct((M, B_ROPE), F32)],
        compiler_params=_cparams(("parallel",)), name="prep_even")(
            z, z, z, z, *tab_a, *tab_r, g_ka, g_qlb, g_ckv, g_kr)


def _prep_odd(z, tab_c, g_kc, tm):
    M = z.shape[0]
    nt = tab_c[0].shape[0] // tm
    return pl.pallas_call(
        _prep_odd_kernel, grid=(M // tm,),
        in_specs=[pl.BlockSpec((tm, LANES), lambda i: (i, ODD_KC // LANES))] + _table_specs(tm, nt)
        + [_full((1, LANES))],
        out_specs=pl.BlockSpec((tm, LANES), lambda i: (i, 0)),
        out_shape=jax.ShapeDtypeStruct((M, LANES), F32),
        compiler_params=_cparams(("parallel",)), name="prep_odd")(z, *tab_c, g_kc)


def _kmean_kernel(*refs, n_in, paged):
    refs = refs[1:] if paged else refs
    o_ref = refs[n_in]
    rows = jnp.concatenate([r[...] for r in refs[:n_in]], axis=0) if n_in > 1 else refs[0][...]
    nb = rows.shape[0] // MOBA_BLOCK
    o_ref[...] = jnp.sum(rows.reshape(nb, MOBA_BLOCK, rows.shape[1]), axis=1) * (1.0 / MOBA_BLOCK)


def _kmean_prompt(kc, B, T):
    nb = T // MOBA_BLOCK
    out = pl.pallas_call(
        functools.partial(_kmean_kernel, n_in=1, paged=False), grid=(B,),
        in_specs=[pl.BlockSpec((T, C_DIM), lambda b: (b, 0))],
        out_specs=pl.BlockSpec((None, nb, C_DIM), lambda b: (b, 0, 0)),
        out_shape=jax.ShapeDtypeStruct((B, nb, C_DIM), F32),
        compiler_params=_cparams(("parallel",)), name="kmean_prompt")(kc)
    return out


def _kmean_sample(pool, layer, page_table, pages_per_step):
    Bs, n_pages = page_table.shape
    pps = pages_per_step
    steps = n_pages // pps
    nb_step = pps * PAGE_SIZE // MOBA_BLOCK

    def page_spec(j):
        return pl.BlockSpec((None, None, PAGE_SIZE, C_DIM), lambda b, c, pt: (layer, pt[b, c * pps + j], 0, 0))

    gs = pltpu.PrefetchScalarGridSpec(
        num_scalar_prefetch=1, grid=(Bs, steps),
        in_specs=[page_spec(j) for j in range(pps)],
        out_specs=pl.BlockSpec((None, None, nb_step, C_DIM), lambda b, c, pt: (b, c, 0, 0)))
    out = pl.pallas_call(
        functools.partial(_kmean_kernel, n_in=pps, paged=True), grid_spec=gs,
        out_shape=jax.ShapeDtypeStruct((Bs, steps, nb_step, C_DIM), F32),
        compiler_params=_cparams(("parallel", "arbitrary")), name="kmean_sample")(page_table, *([pool] * pps))
    return out.reshape(Bs, steps * nb_step, C_DIM)


def _moba_select(q, kmean, own):
    q_hi, q_lo = _split_bf16(q)
    k_hi, k_lo = _split_bf16(kmean)
    gate = _dot_nt(q_hi, k_hi) + _dot_nt(q_hi, k_lo) + _dot_nt(q_lo, k_hi)
    lane = _lane_iota(gate.shape)
    lane_f = lane.astype(F32)
    valid = lane < own
    gate = jnp.where(valid, gate, -jnp.inf)
    sel = jnp.zeros(gate.shape, F32)
    for _ in range(MOBA_TOPK):
        best = jnp.max(gate, axis=-1, keepdims=True)
        first = jnp.min(jnp.where(gate == best, lane_f, float(LANES)), axis=-1, keepdims=True)
        pick = lane_f == first
        sel = jnp.where(pick & valid, 1.0, sel)
        gate = jnp.where(pick, -jnp.inf, gate)
    return sel


def _online_softmax(s, allowed, v_bf16, m_ref, l_ref, acc_ref):
    s = jnp.where(allowed, s, MASKED_SCORE)
    m_old = m_ref[...]
    m_new = jnp.maximum(m_old, jnp.max(s, axis=-1, keepdims=True))
    alpha = jnp.exp(m_old - m_new)
    p = jnp.exp(s - m_new)
    l_ref[...] = alpha * l_ref[...] + jnp.sum(p, axis=-1, keepdims=True)
    acc_ref[...] = alpha * acc_ref[...] + _dot(p.astype(BF16), v_bf16)
    m_ref[...] = m_new


def _attn_kernel(*refs, variant, tq, sample, pages, n_chunks, past_len, lam_init):
    M = HEADS * tq
    it = iter(refs)
    if sample:
        next(it)
    q_ref = next(it)
    k_ref = next(it)
    v_ref = next(it)
    if sample:
        kp_refs = [next(it) for _ in range(pages)]
        vp_refs = [next(it) for _ in range(pages)]
    tab = None
    if variant in "abc":
        tab = (next(it), next(it), next(it))
    if variant == "a":
        gq_ref, lam_ref, gsub_ref = next(it), next(it), next(it)
    elif variant == "b":
        gq_ref, wuk_ref, wuv_ref = next(it), next(it), next(it)
    elif variant == "c":
        gq_ref, kmean_ref = next(it), next(it)
    else:
        tri_ref = next(it)
    o_ref = next(it)
    q_scr = next(it)
    m1, l1, acc1 = next(it), next(it), next(it)
    if variant == "a":
        m2, l2, acc2 = next(it), next(it), next(it)
    if variant == "c":
        sel_scr = next(it)

    if sample:
        step = pl.program_id(1)
        q_base = past_len
        last_step = n_chunks - 1
    else:
        qi = pl.program_id(1)
        step = pl.program_id(2)
        q_base = qi * tq
        last_step = n_chunks - 1

    row = lax.broadcasted_iota(jnp.int32, (M, 1), 0)
    qpos = q_base + (row & (tq - 1))
    own = lax.shift_right_arithmetic(qpos, MOBA_SHIFT)

    @pl.when(step == 0)
    def _init():
        t = None if tab is None else tuple(r[...] for r in tab)
        pieces, sels = [], []
        for h in range(HEADS):
            if variant == "a":
                y = _norm_pair64(q_ref[:, h * LANES:(h + 1) * LANES], gq_ref[...])
                pieces.append(_rope(y, t, A_ROT // 2) * (A_DIM ** -0.5))
            elif variant == "b":
                x = q_ref[:, h * B_HEAD_SLAB:(h + 1) * B_HEAD_SLAB]
                y = _norm_rows(x, gq_ref[...], B_QK)
                lat = _dot(y[:, :B_NOPE].astype(BF16), wuk_ref[h])
                rp = _rope(y[:, B_NOPE:], t, B_ROPE // 2)
                pieces.append(jnp.concatenate([lat, rp], axis=1) * (B_QK ** -0.5))
            elif variant == "c":
                y = _norm_rows(q_ref[:, h * LANES:(h + 1) * LANES], gq_ref[...], C_DIM)
                y = _rope(y, t, C_ROT // 2)
                pieces.append(y * (C_DIM ** -0.5))
                own_t = lax.shift_right_arithmetic(q_base + lax.broadcasted_iota(jnp.int32, (tq, 1), 0), MOBA_SHIFT)
                sels.append(_moba_select(y, kmean_ref[...], own_t))
            else:
                pieces.append(q_ref[:, h * LANES:(h + 1) * LANES] * (D_DIM ** -0.5))
        q_scr[...] = jnp.concatenate(pieces, axis=0).astype(BF16)
        if variant == "c":
            sel_scr[...] = jnp.concatenate(sels, axis=0)
        if variant == "d":
            m1[...] = jnp.zeros(m1.shape, F32)
        else:
            m1[...] = jnp.full(m1.shape, MASKED_SCORE, F32)
            l1[...] = jnp.zeros(l1.shape, F32)
        acc1[...] = jnp.zeros(acc1.shape, F32)
        if variant == "a":
            m2[...] = jnp.full(m2.shape, MASKED_SCORE, F32)
            l2[...] = jnp.zeros(l2.shape, F32)
            acc2[...] = jnp.zeros(acc2.shape, F32)

    def block(k, v, k_base):
        n = k.shape[0]
        kpos = k_base + lax.broadcasted_iota(jnp.int32, (M, n), 1)
        q = q_scr[...]
        if variant == "a":
            lo = _lane_iota(k.shape) < A_DIM
            allowed = kpos <= qpos
            vb = v.astype(BF16)
            _online_softmax(_dot_nt(q, jnp.where(lo, k, 0.0).astype(BF16)), allowed, vb, m1, l1, acc1)
            _online_softmax(_dot_nt(q, jnp.where(lo, 0.0, k).astype(BF16)), allowed, vb, m2, l2, acc2)
        elif variant == "b":
            kb = k.astype(BF16)
            s = _dot_nt(q[:, :B_KV_LORA], kb) + _dot_nt(q[:, B_KV_LORA:B_KV_LORA + B_ROPE], v.astype(BF16))
            _online_softmax(s, kpos <= qpos, kb, m1, l1, acc1)
        elif variant == "c":
            slot = k_base // MOBA_BLOCK
            hit = jnp.sum(jnp.where(_lane_iota(sel_scr.shape) == slot, sel_scr[...], 0.0), axis=-1, keepdims=True)
            limit = jnp.where(own == slot, qpos, jnp.where(hit > 0.5, ALL_KEYS, -1))
            _online_softmax(_dot_nt(q, k.astype(BF16)), kpos <= limit, v.astype(BF16), m1, l1, acc1)
        else:
            z = _dot_nt(q, k.astype(BF16))
            strict = kpos < qpos
            log_beta = jnp.minimum(z, 0.0) - jnp.log1p(jnp.exp(-jnp.abs(z)))
            log_1m = jnp.where(strict, log_beta - z, 0.0)
            hi, lo = _split_bf16(log_1m)
            tri = tri_ref[:n, :n]
            tail = m1[...] + _dot(hi, tri) + _dot(lo, tri)
            a = jnp.where(strict, jnp.exp(log_beta + tail), 0.0)
            acc1[...] += _dot(a.astype(BF16), v.astype(BF16))
            m1[...] += jnp.sum(log_1m, axis=-1, keepdims=True)

    def chunk(k, v, k_base):
        nblk = k.shape[0] // KEY_BLOCK
        if variant in "ab" or nblk <= 1:
            block(k, v, k_base)
        else:
            for s in reversed(range(nblk)):
                block(k[s * KEY_BLOCK:(s + 1) * KEY_BLOCK], v[s * KEY_BLOCK:(s + 1) * KEY_BLOCK],
                      k_base + s * KEY_BLOCK)

    if sample:
        @pl.when(step == 0)
        def _new_rows():
            pad = PAGE_SIZE - tq
            k = jnp.concatenate([k_ref[...], jnp.zeros((pad, k_ref.shape[1]), F32)], axis=0)
            v = jnp.concatenate([v_ref[...], jnp.zeros((pad, v_ref.shape[1]), F32)], axis=0)
            block(k, v, past_len)

        ch = n_chunks - 1 - step
        k = jnp.concatenate([r[...] for r in kp_refs], axis=0)
        v = jnp.concatenate([r[...] for r in vp_refs], axis=0)
        chunk(k, v, ch * (pages * PAGE_SIZE))
    else:
        @pl.when(step <= qi)
        def _blk():
            block(k_ref[...], v_ref[...], (qi - step) * KEY_BLOCK)

    @pl.when(step == last_step)
    def _fin():
        if variant == "a":
            lp = lam_ref[...]
            lam = (jnp.exp(jnp.sum(lp[0:1] * lp[1:2], axis=-1, keepdims=True))
                   - jnp.exp(jnp.sum(lp[2:3] * lp[3:4], axis=-1, keepdims=True)) + lam_init)
            o = acc1[...] / l1[...] - lam * (acc2[...] / l2[...])
            o = o * lax.rsqrt(jnp.mean(o * o, axis=-1, keepdims=True) + EPS) * gsub_ref[...] * (1.0 - lam_init)
        elif variant == "d":
            o = acc1[...]
        else:
            o = acc1[...] / l1[...]
        for h in range(HEADS):
            oh = o[h * tq:(h + 1) * tq]
            if variant == "b":
                oh = _dot(oh.astype(BF16), wuv_ref[h])
            o_ref[:, h * LANES:(h + 1) * LANES] = oh.astype(o_ref.dtype)


def _attn_scratch(variant, M):
    dq = B_Q_WIDTH if variant == "b" else LANES
    dv = B_KV_LORA if variant == "b" else LANES
    state = [pltpu.VMEM((M, 1), F32), pltpu.VMEM((M, 1), F32), pltpu.VMEM((M, dv), F32)]
    scr = [pltpu.VMEM((M, dq), BF16)] + state
    if variant == "a":
        scr += state
    if variant == "c":
        scr.append(pltpu.VMEM((M, LANES), F32))
    return scr


def _attn_prompt(variant, q_arr, q_col, q_width, k_arr, k_col, k_width, v_arr, v_col, v_width, params, B, T,
                 lam_init=0.0):
    tq = min(KEY_BLOCK, T)
    nq = T // tq
    M = HEADS * tq

    def kv_spec(col, width):
        return pl.BlockSpec((tq, width), lambda b, qi, ki: (b * nq + jnp.maximum(qi - ki, 0), col))

    in_specs = [pl.BlockSpec((tq, q_width), lambda b, qi, ki: (b * nq + qi, q_col)),
                kv_spec(k_col, k_width), kv_spec(v_col, v_width)]
    args = [q_arr, k_arr, v_arr]
    for arr, kind in params:
        if kind == "tab":
            in_specs.append(pl.BlockSpec((tq, LANES), lambda b, qi, ki: (qi, 0)))
        elif kind == "batch":
            in_specs.append(pl.BlockSpec((None,) + arr.shape[1:], lambda b, qi, ki: (b,) + (0,) * (arr.ndim - 1)))
        else:
            in_specs.append(_full(arr.shape))
        args.append(arr)
    kern = functools.partial(_attn_kernel, variant=variant, tq=tq, sample=False, pages=0, n_chunks=nq,
                             past_len=0, lam_init=lam_init)
    return pl.pallas_call(
        kern, grid=(B, nq, nq), in_specs=in_specs,
        out_specs=pl.BlockSpec((tq, HEADS * LANES), lambda b, qi, ki: (b * nq + qi, 0)),
        out_shape=jax.ShapeDtypeStruct((B * T, HEADS * LANES), BF16),
        scratch_shapes=_attn_scratch(variant, M),
        compiler_params=_cparams(("parallel", "parallel", "arbitrary")), name="attn_prompt_" + variant)(*args)


def _attn_sample(variant, q_arr, q_col, q_width, k_arr, k_col, k_width, v_arr, v_col, v_width,
                 k_pool, v_pool, layer, page_table, params, pages, lam_init=0.0):
    Bs, Ts = q_arr.shape[:2]
    n_pages = page_table.shape[1]
    pages = min(pages, n_pages)
    n_chunks = n_pages // pages
    M = HEADS * Ts

    def new_spec(col, width):
        return pl.BlockSpec((None, Ts, width), lambda b, c, pt: (b, 0, col))

    def page_spec(width, j):
        return pl.BlockSpec((None, None, PAGE_SIZE, width),
                            lambda b, c, pt: (layer, pt[b, (n_chunks - 1 - c) * pages + j], 0, 0))

    in_specs = [new_spec(q_col, q_width), new_spec(k_col, k_width), new_spec(v_col, v_width)]
    in_specs += [page_spec(k_pool.shape[-1], j) for j in range(pages)]
    in_specs += [page_spec(v_pool.shape[-1], j) for j in range(pages)]
    args = [q_arr, k_arr, v_arr] + [k_pool] * pages + [v_pool] * pages
    for arr, kind in params:
        if kind == "batch":
            in_specs.append(pl.BlockSpec((None,) + arr.shape[1:], lambda b, c, pt: (b,) + (0,) * (arr.ndim - 1)))
        else:
            in_specs.append(_full(arr.shape))
        args.append(arr)
    gs = pltpu.PrefetchScalarGridSpec(
        num_scalar_prefetch=1, grid=(Bs, n_chunks), in_specs=in_specs,
        out_specs=pl.BlockSpec((None, Ts, HEADS * LANES), lambda b, c, pt: (b, 0, 0)),
        scratch_shapes=_attn_scratch(variant, M))
    kern = functools.partial(_attn_kernel, variant=variant, tq=Ts, sample=True, pages=pages, n_chunks=n_chunks,
                             past_len=n_pages * PAGE_SIZE, lam_init=lam_init)
    return pl.pallas_call(
        kern, grid_spec=gs, out_shape=jax.ShapeDtypeStruct((Bs, Ts, HEADS * LANES), F32),
        compiler_params=_cparams(("parallel", "arbitrary")), name="attn_sample_" + variant)(page_table, *args)


def _pad_last(a, n):
    return jnp.pad(a, [(0, 0)] * (a.ndim - 1) + [(0, n - a.shape[-1])])


def kernel(x_prompt, x_sample, cache_a_k, cache_a_v, cache_b_ckv, cache_b_kr, cache_c_k, cache_c_v, cache_d_k, cache_d_v, page_table, c_prompt, c_sample, w_ada, b_ada, ada_table, w_in_even, a_q_norm, a_k_norm, a_lambda_q1, a_lambda_k1, a_lambda_q2, a_lambda_k2, a_sub_norm, b_q_lat_norm, b_w_q_up, b_q_norm, b_kv_norm, b_kr_norm, b_w_uk, b_w_uv, w_out_even, w_in_odd, c_q_norm, c_k_norm, w_out_odd, w_mlp_up, w_mlp_down):
    B, T, D = x_prompt.shape
    Bs, Ts, _ = x_sample.shape
    depth = ada_table.shape[0]
    n_pages = page_table.shape[1]
    past_len = n_pages * PAGE_SIZE
    Mp, Ms = B * T, Bs * Ts
    n_even = w_in_even.shape[0]
    n_odd = w_in_odd.shape[0]

    wie = w_in_even
    w_in_e = jnp.concatenate(
        [wie[..., 0:2048], wie[..., 2304:3328], wie[..., 2048:2304], wie[..., 3328:3648],
         jnp.zeros(wie.shape[:2] + (EVEN_WIDTH - 3648,), wie.dtype)], axis=-1).astype(BF16)
    wio = w_in_odd
    w_in_o = jnp.concatenate([wio[..., 0:2048], wio[..., 2304:4352], wio[..., 2048:2304], wio[..., 4352:4608]],
                             axis=-1).astype(BF16)
    w_qup = _pad_last(b_w_q_up, B_HEAD_SLAB).reshape(n_even, B_Q_LORA, HEADS * B_HEAD_SLAB).astype(BF16)
    w_uk_t = jnp.transpose(b_w_uk, (0, 2, 3, 1)).astype(BF16)
    w_uv = jnp.transpose(b_w_uv, (0, 2, 1, 3)).astype(BF16)
    w_out_e = w_out_even.astype(BF16)
    w_out_o = w_out_odd.astype(BF16)
    w_up = w_mlp_up.astype(BF16)
    w_down = w_mlp_down.astype(BF16)
    w_ada_b = w_ada.astype(BF16)[None]

    g_qa = jnp.tile(a_q_norm, (1, 2))[:, None, :]
    g_ka = jnp.tile(a_k_norm, (1, 2))[:, None, :]
    g_sub = a_sub_norm[:, None, :]
    g_qlb = b_q_lat_norm[:, None, :]
    g_bq = _pad_last(b_q_norm, B_HEAD_SLAB)[:, None, :]
    g_ckv = b_kv_norm[:, None, :]
    g_kr = _pad_last(b_kr_norm, LANES)[:, None, :]
    g_qc = c_q_norm[:, None, :]
    g_kc = c_k_norm[:, None, :]
    lam_rows = jnp.stack([a_lambda_q1, a_lambda_k1, a_lambda_q2, a_lambda_k2], axis=1)
    lam_rows = jnp.pad(lam_rows, ((0, 0), (0, 4), (0, LANES - A_DIM)))
    tri = jnp.asarray(np.tril(np.ones((KEY_BLOCK, KEY_BLOCK), np.float32), -1), BF16)

    pos_p = jnp.arange(T)
    pos_s = past_len + jnp.arange(Ts)
    tm_p = _pick_tile(Mp, 256)
    tm_s = _pick_tile(Ms, 256)

    def tables(pos, tile_rows):
        reps = max(1, tile_rows // pos.shape[0])
        mk = lambda rot, width, period: tuple(jnp.tile(t, (reps, 1)) for t in _rope_tables(pos, rot, width, period))
        return mk(A_ROT, LANES, A_DIM), mk(B_ROPE, B_ROPE, LANES), mk(C_ROT, LANES, LANES)

    tabs_p = tables(pos_p, tm_p)
    tabs_s = tables(pos_s, tm_s)
    tabs_s8 = tables(pos_s, Ts)

    c_all = jnp.concatenate([c_prompt, c_sample], axis=0)
    n_c = c_all.shape[0]
    c_all = jnp.pad(c_all, ((0, (-n_c) % 16), (0, 0)))
    base = _matmul(c_all, w_ada_b, 0, out_dtype=F32, tm=c_all.shape[0], tn=1024, silu=True,
                   bias=b_ada[None, :])[:n_c].reshape(n_c, ADA_CHUNKS, D)

    xp = x_prompt.reshape(Mp, D)
    xs = x_sample.reshape(Ms, D)
    new_rows = {n: ([], []) for n in ("a_k", "a_v", "b_ckv", "b_kr", "c_k", "c_v", "d_k", "d_v")}

    for l in range(depth):
        i = l // 2
        mod = base + ada_table[l]
        mod_p = mod[:B].reshape(B * ADA_CHUNKS, 1, D)
        mod_s = jnp.repeat(jnp.transpose(mod[B:], (1, 0, 2)), Ts, axis=1)

        def p_spec(chunk, width=None):
            def make(tm, tn=D):
                per_batch = T // tm
                return pl.BlockSpec((None, 1, tn), lambda r, *jk: ((r // per_batch) * ADA_CHUNKS + chunk, 0,
                                                                    jk[0] if jk else 0))
            return make

        def s_spec(chunk):
            def make(tm, tn=D):
                return pl.BlockSpec((None, tm, tn), lambda r, *jk: (chunk, r, jk[0] if jk else 0))
            return make

        groups = (("p", xp, mod_p, p_spec), ("s", xs, mod_s, s_spec))
        new_x = {}
        for name, x, md, spec in groups:
            h = _prenorm(x, md, spec(0), spec(1))
            if l % 2 == 0:
                z = _matmul(h, w_in_e, i, out_dtype=F32)
                tabs = tabs_p if name == "p" else tabs_s
                tm = tm_p if name == "p" else tm_s
                qlbn, ka, ckv, kr = _prep_even(z, tabs[0], tabs[1], g_ka[i], g_qlb[i], g_ckv[i], g_kr[i], tm)
                qh = _matmul(qlbn, w_qup, i, out_dtype=F32)
                lam_init = 0.8 - 0.6 * math.exp(-0.3 * l)
                if name == "p":
                    pa = [(t, "tab") for t in tabs_p[0]] + [(g_qa[i], "full"), (lam_rows[i], "full"), (g_sub[i], "full")]
                    o_a = _attn_prompt("a", z, 0, 2048, ka, 0, LANES, z, EVEN_VA // LANES, LANES, pa, B, T, lam_init)
                    pb = [(t, "tab") for t in tabs_p[1]] + [(g_bq[i], "full"), (w_uk_t[i], "full"), (w_uv[i], "full")]
                    o_b = _attn_prompt("b", qh, 0, HEADS * B_HEAD_SLAB, ckv, 0, B_KV_LORA, kr, 0, B_ROPE, pb, B, T)
                    va = z[:, EVEN_VA:EVEN_VA + LANES]
                else:
                    z3 = z.reshape(Bs, Ts, EVEN_WIDTH)
                    pa = [(t, "full") for t in tabs_s8[0]] + [(g_qa[i], "full"), (lam_rows[i], "full"), (g_sub[i], "full")]
                    o_a = _attn_sample("a", z3, 0, 2048, ka.reshape(Bs, Ts, LANES), 0, LANES, z3, EVEN_VA // LANES,
                                       LANES, cache_a_k, cache_a_v, i, page_table, pa, 8, lam_init)
                    pb = [(t, "full") for t in tabs_s8[1]] + [(g_bq[i], "full"), (w_uk_t[i], "full"), (w_uv[i], "full")]
                    o_b = _attn_sample("b", qh.reshape(Bs, Ts, -1), 0, HEADS * B_HEAD_SLAB,
                                       ckv.reshape(Bs, Ts, B_KV_LORA), 0, B_KV_LORA, kr.reshape(Bs, Ts, B_ROPE), 0,
                                       B_ROPE, cache_b_ckv, cache_b_kr, i, page_table, pb, 8)
                    o_a = o_a.reshape(Ms, -1).astype(BF16)
                    o_b = o_b.reshape(Ms, -1).astype(BF16)
                    va = z[:, EVEN_VA:EVEN_VA + LANES]
                o = jnp.concatenate([o_a, o_b], axis=1)
                w_out = w_out_e
                outs = (("a_k", ka), ("a_v", va), ("b_ckv", ckv), ("b_kr", kr))
            else:
                z = _matmul(h, w_in_o, i, out_dtype=F32)
                tabs = tabs_p if name == "p" else tabs_s
                tm = tm_p if name == "p" else tm_s
                kc = _prep_odd(z, tabs[2], g_kc[i], tm)
                vc = z[:, ODD_VC:ODD_VC + LANES]
                kd = z[:, ODD_KD:ODD_KD + LANES]
                vd = z[:, ODD_VD:ODD_VD + LANES]
                if name == "p":
                    kmean = _kmean_prompt(kc, B, T)
                    kmean = jnp.pad(kmean, ((0, 0), (0, LANES - kmean.shape[1]), (0, 0)))
                    pc = [(t, "tab") for t in tabs_p[2]] + [(g_qc[i], "full"), (kmean, "batch")]
                    o_c = _attn_prompt("c", z, 0, 2048, kc, 0, LANES, z, ODD_VC // LANES, LANES, pc, B, T)
                    o_d = _attn_prompt("d", z, 1, 2048, z, ODD_KD // LANES, LANES, z, ODD_VD // LANES, LANES,
                                       [(tri, "full")], B, T)
                else:
                    z3 = z.reshape(Bs, Ts, ODD_WIDTH)
                    kmean = _kmean_sample(cache_c_k, i, page_table, min(16, n_pages))
                    kmean = jnp.pad(kmean, ((0, 0), (0, LANES - kmean.shape[1]), (0, 0)))
                    pc = [(t, "full") for t in tabs_s8[2]] + [(g_qc[i], "full"), (kmean, "batch")]
                    o_c = _attn_sample("c", z3, 0, 2048, kc.reshape(Bs, Ts, LANES), 0, LANES, z3, ODD_VC // LANES,
                                       LANES, cache_c_k, cache_c_v, i, page_table, pc, 8)
                    o_d = _attn_sample("d", z3, 1, 2048, z3, ODD_KD // LANES, LANES, z3, ODD_VD // LANES, LANES,
                                       cache_d_k, cache_d_v, i, page_table, [(tri, "full")], 8)
                    o_c = o_c.reshape(Ms, -1).astype(BF16)
                    o_d = o_d.reshape(Ms, -1).astype(BF16)
                o = jnp.concatenate([o_c, o_d], axis=1)
                w_out = w_out_o
                outs = (("c_k", kc), ("c_v", vc), ("d_k", kd), ("d_v", vd))
            for n, r in outs:
                new_rows[n][0 if name == "p" else 1].append(r)
            x = _matmul(o, w_out, i, out_dtype=F32, tn=512, res=x, gate=md, gate_spec=spec(2))
            h2 = _prenorm(x, md, spec(3), spec(4))
            u = _matmul(h2, w_up, l, out_dtype=BF16, relu2=True)
            x = _matmul(u, w_down, l, out_dtype=F32, tk=2048, res=x, gate=md, gate_spec=spec(5))
            new_x[name] = x
        xp, xs = new_x["p"], new_x["s"]

    def stack(n):
        rp, rs = new_rows[n]
        return (jnp.stack(rp).reshape(len(rp), B, T, -1), jnp.stack(rs).reshape(len(rs), Bs, Ts, -1))

    out = [xp.reshape(B, T, D), xs.reshape(Bs, Ts, D)]
    for n in ("a_k", "a_v", "b_ckv", "b_kr", "c_k", "c_v", "d_k", "d_v"):
        out.extend(stack(n))
    return tuple(out)
```

```python
import functools
import math

import numpy as np
import jax
import jax.numpy as jnp
from jax import lax
from jax.experimental import pallas as pl
from jax.experimental.pallas import tpu as pltpu

F32 = jnp.float32
BF16 = jnp.bfloat16

EPS = 1e-6
ROPE_THETA = 500000.0
ADA_CHUNKS = 6
PAGE_SIZE = 128
MASKED_SCORE = -1e30
ALL_KEYS = 2 ** 30

LANES = 128
HEADS = 16
A_DIM = 64
A_ROT = 16
B_NOPE = 128
B_ROPE = 64
B_QK = B_NOPE + B_ROPE
B_HEAD_SLAB = 256
B_Q_LORA = 1024
B_KV_LORA = 256
B_Q_WIDTH = B_KV_LORA + LANES
C_DIM = 128
C_ROT = 32
D_DIM = 128
MOBA_BLOCK = 256
MOBA_TOPK = 3
MOBA_SHIFT = 8
KEY_BLOCK = 256
ROW_TILE = {"a": 128, "b": 256, "c": 128, "d": 128}
TILES_PER_TRIP = {"a": 32, "b": 4, "c": 16, "d": 16}
MAX_ROW_TILE = 256
SAMPLE_PAGES_PER_STEP = 16
SUFFIX_BLOCK = 128
LOG2_E = 1.4426950408889634
VMEM_LIMIT_BYTES = 56 * 1024 * 1024

EVEN_QA, EVEN_QLB, EVEN_KA, EVEN_VA, EVEN_CKV, EVEN_KR, EVEN_WIDTH = 0, 2048, 3072, 3200, 3328, 3584, 3840
ODD_QC, ODD_QD, ODD_KC, ODD_VC, ODD_KD, ODD_VD, ODD_WIDTH = 0, 2048, 4096, 4224, 4352, 4480, 4608


def _cparams(sem):
    return pltpu.CompilerParams(dimension_semantics=sem, vmem_limit_bytes=VMEM_LIMIT_BYTES)


def _dot(a, b):
    return jnp.dot(a, b, preferred_element_type=F32)


def _dot_nt(a, b):
    return lax.dot_general(a, b, (((1,), (1,)), ((), ())), preferred_element_type=F32)


def _split_bf16(x):
    hi = x.astype(BF16)
    lo = (x - hi.astype(F32)).astype(BF16)
    return hi, lo


def _mm_kernel(*refs, nk, silu, has_bias, relu2, has_res):
    it = iter(refs)
    x_ref = next(it)
    w_ref = next(it)
    b_ref = next(it) if has_bias else None
    r_ref = next(it) if has_res else None
    g_ref = next(it) if has_res else None
    o_ref = next(it)
    acc_ref = next(it) if nk > 1 else None

    x = x_ref[...]
    if silu:
        x = (x * (1.0 / (1.0 + jnp.exp(-x)))).astype(BF16)
    part = _dot(x, w_ref[...])

    def epilogue(acc):
        if has_bias:
            acc = acc + b_ref[...]
        if relu2:
            acc = jnp.maximum(acc, 0.0)
            acc = acc * acc
        if has_res:
            acc = r_ref[...] + g_ref[...] * acc
        o_ref[...] = acc.astype(o_ref.dtype)

    if nk == 1:
        epilogue(part)
    else:
        k = pl.program_id(2)

        @pl.when(k == 0)
        def _():
            acc_ref[...] = part

        @pl.when(k > 0)
        def _():
            acc_ref[...] += part

        @pl.when(k == nk - 1)
        def _():
            epilogue(acc_ref[...])


def _pick_tile(n, pref, unit=LANES):
    if n <= pref:
        return n
    t = pref - pref % unit
    while n % t:
        t -= unit
    return t


def _matmul(x, w, layer, *, out_dtype, tm=1024, tn=1024, tk=4096, silu=False, bias=None, relu2=False,
            res=None, gate=None, gate_spec=None):
    M, K = x.shape
    N = w.shape[-1]
    tm, tn, tk = _pick_tile(M, tm), _pick_tile(N, tn), _pick_tile(K, tk)
    nk = K // tk
    in_specs = [pl.BlockSpec((tm, tk), lambda i, j, k: (i, k)),
                pl.BlockSpec((None, tk, tn), lambda i, j, k: (layer, k, j))]
    args = [x, w]
    if bias is not None:
        in_specs.append(pl.BlockSpec((1, tn), lambda i, j, k: (0, j)))
        args.append(bias)
    if res is not None:
        in_specs.append(pl.BlockSpec((tm, tn), lambda i, j, k: (i, j)))
        in_specs.append(gate_spec(tm, tn))
        args += [res, gate]
    scratch = [pltpu.VMEM((tm, tn), F32)] if nk > 1 else []
    kern = functools.partial(_mm_kernel, nk=nk, silu=silu, has_bias=bias is not None, relu2=relu2,
                             has_res=res is not None)
    return pl.pallas_call(
        kern, grid=(M // tm, N // tn, nk), in_specs=in_specs,
        out_specs=pl.BlockSpec((tm, tn), lambda i, j, k: (i, j)),
        out_shape=jax.ShapeDtypeStruct((M, N), out_dtype), scratch_shapes=scratch,
        compiler_params=_cparams(("parallel", "parallel", "arbitrary")), name="matmul")(*args)


def _prenorm_kernel(x_ref, shift_ref, scale_ref, o_ref):
    x = x_ref[...]
    y = x * lax.rsqrt(jnp.mean(x * x, axis=-1, keepdims=True) + EPS)
    o_ref[...] = (y * (1.0 + scale_ref[...]) + shift_ref[...]).astype(o_ref.dtype)


def _prenorm(x, mod, shift_spec, scale_spec, tm=256):
    M, D = x.shape
    tm = _pick_tile(M, tm)
    return pl.pallas_call(
        _prenorm_kernel, grid=(M // tm,),
        in_specs=[pl.BlockSpec((tm, D), lambda i: (i, 0)), shift_spec(tm), scale_spec(tm)],
        out_specs=pl.BlockSpec((tm, D), lambda i: (i, 0)),
        out_shape=jax.ShapeDtypeStruct((M, D), BF16),
        compiler_params=_cparams(("parallel",)), name="prenorm")(x, mod, mod)


def _lane_iota(shape):
    return lax.broadcasted_iota(jnp.int32, shape, len(shape) - 1)


def _rope(y, tab, half):
    c, s_up, s_dn = tab
    return y * c + pltpu.roll(y, LANES - half, 1) * s_up + pltpu.roll(y, half, 1) * s_dn


def _norm_pair64(x, gain):
    lo = _lane_iota(x.shape) < A_DIM
    sq = x * x
    s_lo = jnp.sum(jnp.where(lo, sq, 0.0), axis=-1, keepdims=True)
    s_hi = jnp.sum(jnp.where(lo, 0.0, sq), axis=-1, keepdims=True)
    inv = jnp.where(lo, lax.rsqrt(s_lo * (1.0 / A_DIM) + EPS), lax.rsqrt(s_hi * (1.0 / A_DIM) + EPS))
    return x * inv * gain


def _norm_rows(x, gain, n_real):
    ms = jnp.sum(x * x, axis=-1, keepdims=True) * (1.0 / n_real)
    return x * lax.rsqrt(ms + EPS) * gain


def _rope_tables(pos, rot, width, period):
    half = rot // 2
    inv = ROPE_THETA ** (-jnp.arange(half, dtype=F32) * 2.0 / rot)
    ang = pos.astype(F32)[:, None] * inv[None, :]
    cos, sin = jnp.cos(ang), jnp.sin(ang)
    lane = np.arange(LANES)
    g = lane % period
    first = (g < half) & (lane < width)
    second = (g >= half) & (g < rot) & (lane < width)
    idx = np.where(first, g, np.where(second, g - half, 0))
    cos_l, sin_l = cos[:, idx], sin[:, idx]
    rot_l = jnp.asarray(first | second)
    c = jnp.where(rot_l[None, :], cos_l, 1.0)
    s_up = jnp.where(jnp.asarray(first)[None, :], -sin_l, 0.0)
    s_dn = jnp.where(jnp.asarray(second)[None, :], sin_l, 0.0)
    return c, s_up, s_dn


def _prep_even_kernel(qlb_ref, ka_ref, ckv_ref, kr_ref, ca, sua, sda, cr, sur, sdr,
                      g_ka, g_qlb, g_ckv, g_kr, qlbn_o, ka_o, ckv_o, kr_o):
    qlb = qlb_ref[...]
    qlbn_o[...] = _norm_rows(qlb, g_qlb[...], B_Q_LORA).astype(qlbn_o.dtype)
    ka = _norm_pair64(ka_ref[...], g_ka[...])
    ka_o[...] = _rope(ka, (ca[...], sua[...], sda[...]), A_ROT // 2)
    ckv_o[...] = _norm_rows(ckv_ref[...], g_ckv[...], B_KV_LORA)
    kr = _norm_rows(kr_ref[...], g_kr[...], B_ROPE)
    kr = _rope(kr, (cr[...], sur[...], sdr[...]), B_ROPE // 2)
    kr_o[...] = kr


def _prep_odd_kernel(kc_ref, cc, suc, sdc, g_kc, kc_o):
    kc = _norm_rows(kc_ref[...], g_kc[...], C_DIM)
    kc_o[...] = _rope(kc, (cc[...], suc[...], sdc[...]), C_ROT // 2)


def _table_specs(tm, n_tab_blocks):
    return [pl.BlockSpec((tm, LANES), lambda i: (i % n_tab_blocks, 0))] * 3


def _full(shape):
    return pl.BlockSpec(shape, lambda *_: (0,) * len(shape))


def _prep_even(z, tab_a, tab_r, g_ka, g_qlb, g_ckv, g_kr, tm):
    M = z.shape[0]
    nt = tab_a[0].shape[0] // tm
    col = lambda width, start: pl.BlockSpec((tm, width), lambda i: (i, start // width))
    row = lambda width: pl.BlockSpec((tm, width), lambda i: (i, 0))
    return pl.pallas_call(
        _prep_even_kernel, grid=(M // tm,),
        in_specs=[col(B_Q_LORA, EVEN_QLB), col(LANES, EVEN_KA), col(B_KV_LORA, EVEN_CKV), col(LANES, EVEN_KR)]
        + _table_specs(tm, nt) + _table_specs(tm, nt)
        + [_full((1, LANES)), _full((1, B_Q_LORA)), _full((1, B_KV_LORA)), _full((1, LANES))],
        out_specs=[row(B_Q_LORA), row(LANES), row(B_KV_LORA), row(LANES)],
        out_shape=[jax.ShapeDtypeStruct((M, B_Q_LORA), BF16), jax.ShapeDtypeStruct((M, LANES), F32),
                   jax.ShapeDtypeStruct((M, B_KV_LORA), F32), jax.ShapeDtypeStruct((M, LANES), F32)],
        compiler_params=_cparams(("parallel",)), name="prep_even")(
            z, z, z, z, *tab_a, *tab_r, g_ka, g_qlb, g_ckv, g_kr)


def _prep_odd(z, tab_c, g_kc, tm):
    M = z.shape[0]
    nt = tab_c[0].shape[0] // tm
    return pl.pallas_call(
        _prep_odd_kernel, grid=(M // tm,),
        in_specs=[pl.BlockSpec((tm, LANES), lambda i: (i, ODD_KC // LANES))] + _table_specs(tm, nt)
        + [_full((1, LANES))],
        out_specs=pl.BlockSpec((tm, LANES), lambda i: (i, 0)),
        out_shape=jax.ShapeDtypeStruct((M, LANES), F32),
        compiler_params=_cparams(("parallel",)), name="prep_odd")(z, *tab_c, g_kc)


def _kmean_kernel(*refs, n_in, paged):
    refs = refs[1:] if paged else refs
    o_ref = refs[n_in]
    rows = jnp.concatenate([r[...] for r in refs[:n_in]], axis=0) if n_in > 1 else refs[0][...]
    nb = rows.shape[0] // MOBA_BLOCK
    o_ref[...] = jnp.sum(rows.reshape(nb, MOBA_BLOCK, rows.shape[1]), axis=1) * (1.0 / MOBA_BLOCK)


def _kmean_prompt(kc, B, T):
    nb = T // MOBA_BLOCK
    out = pl.pallas_call(
        functools.partial(_kmean_kernel, n_in=1, paged=False), grid=(B,),
        in_specs=[pl.BlockSpec((T, C_DIM), lambda b: (b, 0))],
        out_specs=pl.BlockSpec((None, nb, C_DIM), lambda b: (b, 0, 0)),
        out_shape=jax.ShapeDtypeStruct((B, nb, C_DIM), F32),
        compiler_params=_cparams(("parallel",)), name="kmean_prompt")(kc)
    return out


def _kmean_sample(pool, layer, page_table, pages_per_step):
    Bs, n_pages = page_table.shape
    pps = pages_per_step
    steps = n_pages // pps
    nb_step = pps * PAGE_SIZE // MOBA_BLOCK

    def page_spec(j):
        return pl.BlockSpec((None, None, PAGE_SIZE, C_DIM), lambda b, c, pt: (layer, pt[b, c * pps + j], 0, 0))

    gs = pltpu.PrefetchScalarGridSpec(
        num_scalar_prefetch=1, grid=(Bs, steps),
        in_specs=[page_spec(j) for j in range(pps)],
        out_specs=pl.BlockSpec((None, None, nb_step, C_DIM), lambda b, c, pt: (b, c, 0, 0)))
    out = pl.pallas_call(
        functools.partial(_kmean_kernel, n_in=pps, paged=True), grid_spec=gs,
        out_shape=jax.ShapeDtypeStruct((Bs, steps, nb_step, C_DIM), F32),
        compiler_params=_cparams(("parallel", "arbitrary")), name="kmean_sample")(page_table, *([pool] * pps))
    return out.reshape(Bs, steps * nb_step, C_DIM)


def _moba_select(q_t, kmean, own):
    q_hi, q_lo = _split_bf16(q_t)
    k_hi, k_lo = _split_bf16(kmean)
    gate = _dot(k_hi, q_hi) + _dot(k_lo, q_hi) + _dot(k_hi, q_lo)
    slot = lax.broadcasted_iota(jnp.int32, gate.shape, 0)
    slot_f = slot.astype(F32)
    valid = slot < own
    gate = jnp.where(valid, gate, -jnp.inf)
    sel = jnp.zeros(gate.shape, F32)
    for _ in range(MOBA_TOPK):
        best = jnp.max(gate, axis=0, keepdims=True)
        first = jnp.min(jnp.where(gate == best, slot_f, float(gate.shape[0])), axis=0, keepdims=True)
        pick = slot_f == first
        sel = jnp.where(pick & valid, 1.0, sel)
        gate = jnp.where(pick, -jnp.inf, gate)
    return sel


def _online_softmax(s_t, v_t, state):
    m_old, l_old, acc_old = state
    m_new = jnp.maximum(m_old, jnp.max(s_t, axis=0, keepdims=True))
    alpha = jnp.exp2(m_old - m_new)
    p = jnp.exp2(s_t - m_new)
    l_new = alpha * l_old + jnp.sum(p, axis=0, keepdims=True)
    return m_new, l_new, alpha * acc_old + _dot(v_t, p.astype(BF16))


def _suffix_sums(x, tri):
    n, w = x.shape
    pieces, later = [], None
    for b in reversed(range(n // SUFFIX_BLOCK)):
        xb = x[b * SUFFIX_BLOCK:(b + 1) * SUFFIX_BLOCK]
        hi, lo = _split_bf16(xb)
        both = _dot(tri, jnp.concatenate([hi, lo], axis=1))
        s = both[:, :w] + both[:, w:]
        total = jnp.sum(xb, axis=0, keepdims=True)
        if later is not None:
            s, total = s + later, total + later
        pieces.insert(0, s)
        later = total
    return jnp.concatenate(pieces, axis=0), later


def _attn_tiling(variant, M):
    rows = M * (2 if variant == "a" else 1)
    rt = min(ROW_TILE[variant], rows)
    return rt, rows // rt


def _attn_kernel(*refs, variant, tq, sample, pages, n_chunks, past_len, lam_init):
    M = HEADS * tq
    rt, n_tiles = _attn_tiling(variant, M)
    it = iter(refs)
    if sample:
        next(it)
    q_ref = next(it)
    k_ref = next(it)
    v_ref = next(it)
    if sample:
        kp_refs = [next(it) for _ in range(pages)]
        vp_refs = [next(it) for _ in range(pages)]
    tab = None
    if variant in "abc":
        tab = (next(it), next(it), next(it))
    if variant == "a":
        gq_ref, lam_ref, gsub_ref = next(it), next(it), next(it)
    elif variant == "b":
        gq_ref, wuk_ref, wuv_ref = next(it), next(it), next(it)
    elif variant == "c":
        gq_ref, kmean_ref = next(it), next(it)
    else:
        tri_ref = next(it)
    o_ref = next(it)
    q_scr = next(it)
    m1, l1, acc1 = next(it), next(it), next(it)
    if variant == "c":
        sel_scr = next(it)
    state_refs = (m1, acc1) if variant == "d" else (m1, l1, acc1)

    if sample:
        step = pl.program_id(1)
        q_base = past_len
        last_step = n_chunks - 1
    else:
        qi = pl.program_id(1)
        step = pl.program_id(2)
        q_base = qi * tq
        last_step = n_chunks - 1

    q_scale = {"a": A_DIM ** -0.5 * LOG2_E, "b": B_QK ** -0.5 * LOG2_E, "c": C_DIM ** -0.5 * LOG2_E,
               "d": D_DIM ** -0.5}[variant]

    @pl.when(step == 0)
    def _init():
        t = None if tab is None else tuple(r[...] for r in tab)
        pieces, second = [], []
        for h in range(HEADS):
            if variant == "a":
                y = _norm_pair64(q_ref[:, h * LANES:(h + 1) * LANES], gq_ref[...])
                y = _rope(y, t, A_ROT // 2) * q_scale
                lo = _lane_iota(y.shape) < A_DIM
                pieces.append(jnp.where(lo, y, 0.0))
                second.append(jnp.where(lo, 0.0, y))
            elif variant == "b":
                x = q_ref[:, h * B_HEAD_SLAB:(h + 1) * B_HEAD_SLAB]
                y = _norm_rows(x, gq_ref[...], B_QK)
                lat = _dot(y[:, :B_NOPE].astype(BF16), wuk_ref[h])
                rp = _rope(y[:, B_NOPE:], t, B_ROPE // 2)
                pieces.append(jnp.concatenate([lat, rp], axis=1) * q_scale)
            elif variant == "c":
                y = _norm_rows(q_ref[:, h * LANES:(h + 1) * LANES], gq_ref[...], C_DIM)
                y = _rope(y, t, C_ROT // 2)
                pieces.append(y)
            else:
                pieces.append(q_ref[:, h * LANES:(h + 1) * LANES] * q_scale)
        stacked = jnp.concatenate(pieces + second, axis=0)
        for r in range(n_tiles):
            x_t = stacked[r * rt:(r + 1) * rt].T
            if variant == "c":
                row = r * rt + lax.broadcasted_iota(jnp.int32, (1, rt), 1)
                own = lax.shift_right_arithmetic(q_base + (row & (tq - 1)), MOBA_SHIFT)
                sel_scr[r] = _moba_select(x_t, kmean_ref[...], own)
                x_t = x_t * q_scale
            q_scr[r] = x_t.astype(BF16)
        if variant == "d":
            m1[...] = jnp.zeros(m1.shape, F32)
        else:
            m1[...] = jnp.full(m1.shape, MASKED_SCORE, F32)
            l1[...] = jnp.zeros(l1.shape, F32)
        acc1[...] = jnp.zeros(acc1.shape, F32)

    def block(k, v, k_base, causal, v_is_t=False):
        n = k.shape[0]
        if variant == "b":
            kb = k.astype(BF16)
            v_t = k.T.astype(BF16)
            if v_is_t:
                v = jnp.concatenate([v, jnp.zeros((LANES - B_ROPE, n), F32)], axis=0).T
            krb = v.astype(BF16)
        else:
            kb, v_t = k.astype(BF16), v.T.astype(BF16)
        if variant == "c":
            slot = k_base // MOBA_BLOCK
        def tile(r, state):
            q = q_scr[r]
            row = r * rt + lax.broadcasted_iota(jnp.int32, (1, rt), 1)
            qpos = q_base + (row & (tq - 1))
            kpos = k_base + lax.broadcasted_iota(jnp.int32, (n, rt), 0)
            if variant == "a":
                s_t = _dot(kb, q)
                if causal:
                    s_t = jnp.where(kpos <= qpos, s_t, MASKED_SCORE)
                return _online_softmax(s_t, v_t, state)
            if variant == "b":
                s_t = _dot(kb, q[:B_KV_LORA]) + _dot(krb, q[B_KV_LORA:])
                if causal:
                    s_t = jnp.where(kpos <= qpos, s_t, MASKED_SCORE)
                return _online_softmax(s_t, v_t, state)
            if variant == "c":
                own = lax.shift_right_arithmetic(qpos, MOBA_SHIFT)
                sel = sel_scr[r]
                slot_iota = lax.broadcasted_iota(jnp.int32, sel.shape, 0)
                span = min(n, MOBA_BLOCK)
                allowed = []
                for b in range(n // span):
                    hit = jnp.sum(jnp.where(slot_iota == slot + b, sel, 0.0), axis=0, keepdims=True)
                    limit = jnp.where(own == slot + b, qpos, jnp.where(hit > 0.5, ALL_KEYS, -1))
                    allowed.append(kpos[b * span:(b + 1) * span] <= limit)
                allowed = allowed[0] if len(allowed) == 1 else jnp.concatenate(allowed, axis=0)
                s_t = jnp.where(allowed, _dot(kb, q), MASKED_SCORE)
                return _online_softmax(s_t, v_t, state)
            carry, acc = state
            z = _dot(kb, q)
            log_beta = jnp.minimum(z, 0.0) - jnp.log(1.0 + jnp.exp(-jnp.abs(z)))
            log_1m = log_beta - z
            if causal:
                strict = kpos < qpos
                log_1m = jnp.where(strict, log_1m, 0.0)
            newer, total = _suffix_sums(log_1m, tri_ref[...])
            a = jnp.exp(log_beta + (carry + newer))
            if causal:
                a = jnp.where(strict, a, 0.0)
            return carry + total, acc + _dot(v_t, a.astype(BF16))

        group = min(n_tiles, TILES_PER_TRIP[variant])

        def trip(t):
            rs = [t * group + g for g in range(group)]
            states = [tuple(ref[r] for ref in state_refs) for r in rs]
            for r, new in zip(rs, [tile(r, st) for r, st in zip(rs, states)]):
                for ref, val in zip(state_refs, new):
                    ref[r] = val

        if n_tiles == group:
            trip(0)
        else:
            def body(t, carry):
                trip(t)
                return carry
            lax.fori_loop(0, n_tiles // group, body, 0)

    if sample:
        @pl.when(step == 0)
        def _new_rows():
            pad = PAGE_SIZE - tq
            k = jnp.concatenate([k_ref[...], jnp.zeros((pad, k_ref.shape[1]), F32)], axis=0)
            v = jnp.concatenate([v_ref[...], jnp.zeros((pad, v_ref.shape[1]), F32)], axis=0)
            block(k, v, past_len, True)

        ch = n_chunks - 1 - step
        v_is_t = variant == "b"
        k = jnp.concatenate([r[...] for r in kp_refs], axis=0)
        v = jnp.concatenate([r[...] for r in vp_refs], axis=1 if v_is_t else 0)
        block(k, v, ch * (pages * PAGE_SIZE), False, v_is_t)
    else:
        @pl.when(step == 0)
        def _diag():
            block(k_ref[...], v_ref[...], qi * KEY_BLOCK, True)

        @pl.when((step > 0) & (step <= qi))
        def _past():
            block(k_ref[...], v_ref[...], (qi - step) * KEY_BLOCK, False)

    @pl.when(step == last_step)
    def _fin():
        if variant == "a":
            lp = lam_ref[...]
            lam = (jnp.exp(jnp.sum(lp[0:1] * lp[1:2], axis=-1, keepdims=True))
                   - jnp.exp(jnp.sum(lp[2:3] * lp[3:4], axis=-1, keepdims=True)) + lam_init)
        n_out = M // min(rt, M)
        w = min(rt, M)
        for r in range(n_out):
            if variant == "a":
                if n_tiles == 1:
                    both = acc1[0] / l1[0]
                    o_t = both[:, :M] - lam * both[:, M:]
                else:
                    r2 = r + n_out
                    o_t = acc1[r] / l1[r] - lam * (acc1[r2] / l1[r2])
                o_t = (o_t * lax.rsqrt(jnp.mean(o_t * o_t, axis=0, keepdims=True) + EPS)
                       * gsub_ref[:, :w] * (1.0 - lam_init))
            elif variant == "d":
                o_t = acc1[r]
            else:
                o_t = acc1[r] / l1[r]
            o = o_t.T
            for u in range(w // min(tq, w)):
                rows = min(tq, w)
                h, j = divmod(r * w + u * rows, tq)
                oh = o[u * rows:(u + 1) * rows]
                if variant == "b":
                    oh = _dot(oh.astype(BF16), wuv_ref[h])
                o_ref[j:j + rows, h * LANES:(h + 1) * LANES] = oh.astype(o_ref.dtype)


def _attn_scratch(variant, M, params):
    rt, n_tiles = _attn_tiling(variant, M)
    dq = B_Q_WIDTH if variant == "b" else LANES
    dv = B_KV_LORA if variant == "b" else LANES
    scr = [pltpu.VMEM((n_tiles, dq, rt), BF16), pltpu.VMEM((n_tiles, 1, rt), F32), pltpu.VMEM((n_tiles, 1, rt), F32),
           pltpu.VMEM((n_tiles, dv, rt), F32)]
    if variant == "c":
        n_slots = next(a.shape[1] for a, kind in params if kind == "batch")
        scr.append(pltpu.VMEM((n_tiles, n_slots, rt), F32))
    return scr


def _attn_prompt(variant, q_arr, q_col, q_width, k_arr, k_col, k_width, v_arr, v_col, v_width, params, B, T,
                 lam_init=0.0):
    tq = min(KEY_BLOCK, T)
    nq = T // tq
    M = HEADS * tq

    def kv_spec(col, width):
        return pl.BlockSpec((tq, width), lambda b, qi, ki: (b * nq + jnp.maximum(qi - ki, 0), col))

    in_specs = [pl.BlockSpec((tq, q_width), lambda b, qi, ki: (b * nq + qi, q_col)),
                kv_spec(k_col, k_width), kv_spec(v_col, v_width)]
    args = [q_arr, k_arr, v_arr]
    for arr, kind in params:
        if kind == "tab":
            in_specs.append(pl.BlockSpec((tq, LANES), lambda b, qi, ki: (qi, 0)))
        elif kind == "batch":
            in_specs.append(pl.BlockSpec((None,) + arr.shape[1:], lambda b, qi, ki: (b,) + (0,) * (arr.ndim - 1)))
        else:
            in_specs.append(_full(arr.shape))
        args.append(arr)
    kern = functools.partial(_attn_kernel, variant=variant, tq=tq, sample=False, pages=0, n_chunks=nq,
                             past_len=0, lam_init=lam_init)
    return pl.pallas_call(
        kern, grid=(B, nq, nq), in_specs=in_specs,
        out_specs=pl.BlockSpec((tq, HEADS * LANES), lambda b, qi, ki: (b * nq + qi, 0)),
        out_shape=jax.ShapeDtypeStruct((B * T, HEADS * LANES), BF16),
        scratch_shapes=_attn_scratch(variant, M, params),
        compiler_params=_cparams(("parallel", "parallel", "arbitrary")), name="attn_prompt_" + variant)(*args)


def _attn_sample(variant, q_arr, q_col, q_width, k_arr, k_col, k_width, v_arr, v_col, v_width,
                 k_pool, v_pool, layer, page_table, params, pages, lam_init=0.0):
    Bs, Ts = q_arr.shape[:2]
    n_pages = page_table.shape[1]
    pages = min(pages, n_pages)
    n_chunks = n_pages // pages
    M = HEADS * Ts

    def new_spec(col, width):
        return pl.BlockSpec((None, Ts, width), lambda b, c, pt: (b, 0, col))

    def page_spec(pool, j):
        return pl.BlockSpec((None, None) + pool.shape[2:],
                            lambda b, c, pt: (layer, pt[b, (n_chunks - 1 - c) * pages + j], 0, 0))

    in_specs = [new_spec(q_col, q_width), new_spec(k_col, k_width), new_spec(v_col, v_width)]
    in_specs += [page_spec(k_pool, j) for j in range(pages)]
    in_specs += [page_spec(v_pool, j) for j in range(pages)]
    args = [q_arr, k_arr, v_arr] + [k_pool] * pages + [v_pool] * pages
    for arr, kind in params:
        if kind == "batch":
            in_specs.append(pl.BlockSpec((None,) + arr.shape[1:], lambda b, c, pt: (b,) + (0,) * (arr.ndim - 1)))
        else:
            in_specs.append(_full(arr.shape))
        args.append(arr)
    gs = pltpu.PrefetchScalarGridSpec(
        num_scalar_prefetch=1, grid=(Bs, n_chunks), in_specs=in_specs,
        out_specs=pl.BlockSpec((None, Ts, HEADS * LANES), lambda b, c, pt: (b, 0, 0)),
        scratch_shapes=_attn_scratch(variant, M, params))
    kern = functools.partial(_attn_kernel, variant=variant, tq=Ts, sample=True, pages=pages, n_chunks=n_chunks,
                             past_len=n_pages * PAGE_SIZE, lam_init=lam_init)
    return pl.pallas_call(
        kern, grid_spec=gs, out_shape=jax.ShapeDtypeStruct((Bs, Ts, HEADS * LANES), F32),
        compiler_params=_cparams(("parallel", "arbitrary")), name="attn_sample_" + variant)(page_table, *args)


def _pad_last(a, n):
    return jnp.pad(a, [(0, 0)] * (a.ndim - 1) + [(0, n - a.shape[-1])])


def kernel(x_prompt, x_sample, cache_a_k, cache_a_v, cache_b_ckv, cache_b_kr, cache_c_k, cache_c_v, cache_d_k, cache_d_v, page_table, c_prompt, c_sample, w_ada, b_ada, ada_table, w_in_even, a_q_norm, a_k_norm, a_lambda_q1, a_lambda_k1, a_lambda_q2, a_lambda_k2, a_sub_norm, b_q_lat_norm, b_w_q_up, b_q_norm, b_kv_norm, b_kr_norm, b_w_uk, b_w_uv, w_out_even, w_in_odd, c_q_norm, c_k_norm, w_out_odd, w_mlp_up, w_mlp_down):
    B, T, D = x_prompt.shape
    Bs, Ts, _ = x_sample.shape
    depth = ada_table.shape[0]
    n_pages = page_table.shape[1]
    past_len = n_pages * PAGE_SIZE
    Mp, Ms = B * T, Bs * Ts
    n_even = w_in_even.shape[0]
    n_odd = w_in_odd.shape[0]

    wie = w_in_even
    w_in_e = jnp.concatenate(
        [wie[..., 0:2048], wie[..., 2304:3328], wie[..., 2048:2304], wie[..., 3328:3648],
         jnp.zeros(wie.shape[:2] + (EVEN_WIDTH - 3648,), wie.dtype)], axis=-1).astype(BF16)
    wio = w_in_odd
    w_in_o = jnp.concatenate([wio[..., 0:2048], wio[..., 2304:4352], wio[..., 2048:2304], wio[..., 4352:4608]],
                             axis=-1).astype(BF16)
    w_qup = _pad_last(b_w_q_up, B_HEAD_SLAB).reshape(n_even, B_Q_LORA, HEADS * B_HEAD_SLAB).astype(BF16)
    w_uk_t = jnp.transpose(b_w_uk, (0, 2, 3, 1)).astype(BF16)
    w_uv = jnp.transpose(b_w_uv, (0, 2, 1, 3)).astype(BF16)
    w_out_e = w_out_even.astype(BF16)
    w_out_o = w_out_odd.astype(BF16)
    w_up = w_mlp_up.astype(BF16)
    w_down = w_mlp_down.astype(BF16)
    w_ada_b = w_ada.astype(BF16)[None]

    g_qa = jnp.tile(a_q_norm, (1, 2))[:, None, :]
    g_ka = jnp.tile(a_k_norm, (1, 2))[:, None, :]
    g_sub = jnp.broadcast_to(a_sub_norm[:, :, None], a_sub_norm.shape + (MAX_ROW_TILE,))
    g_qlb = b_q_lat_norm[:, None, :]
    g_bq = _pad_last(b_q_norm, B_HEAD_SLAB)[:, None, :]
    g_ckv = b_kv_norm[:, None, :]
    g_kr = _pad_last(b_kr_norm, LANES)[:, None, :]
    g_qc = c_q_norm[:, None, :]
    g_kc = c_k_norm[:, None, :]
    lam_rows = jnp.stack([a_lambda_q1, a_lambda_k1, a_lambda_q2, a_lambda_k2], axis=1)
    lam_rows = jnp.pad(lam_rows, ((0, 0), (0, 4), (0, LANES - A_DIM)))
    tri = jnp.asarray(np.triu(np.ones((SUFFIX_BLOCK, SUFFIX_BLOCK), np.float32), 1), BF16)
    kr_pool_t = jnp.swapaxes(cache_b_kr, 2, 3)

    pos_p = jnp.arange(T)
    pos_s = past_len + jnp.arange(Ts)
    tm_p = _pick_tile(Mp, 256)
    tm_s = _pick_tile(Ms, 256)

    def tables(pos, tile_rows):
        reps = max(1, tile_rows // pos.shape[0])
        mk = lambda rot, width, period: tuple(jnp.tile(t, (reps, 1)) for t in _rope_tables(pos, rot, width, period))
        return mk(A_ROT, LANES, A_DIM), mk(B_ROPE, B_ROPE, LANES), mk(C_ROT, LANES, LANES)

    tabs_p = tables(pos_p, tm_p)
    tabs_s = tables(pos_s, tm_s)
    tabs_s8 = tables(pos_s, Ts)

    c_all = jnp.concatenate([c_prompt, c_sample], axis=0)
    n_c = c_all.shape[0]
    c_all = jnp.pad(c_all, ((0, (-n_c) % 16), (0, 0)))
    base = _matmul(c_all, w_ada_b, 0, out_dtype=F32, tm=c_all.shape[0], tn=1024, silu=True,
                   bias=b_ada[None, :])[:n_c].reshape(n_c, ADA_CHUNKS, D)

    xp = x_prompt.reshape(Mp, D)
    xs = x_sample.reshape(Ms, D)
    new_rows = {n: ([], []) for n in ("a_k", "a_v", "b_ckv", "b_kr", "c_k", "c_v", "d_k", "d_v")}

    for l in range(depth):
        i = l // 2
        mod = base + ada_table[l]
        mod_p = mod[:B].reshape(B * ADA_CHUNKS, 1, D)
        mod_s = jnp.repeat(jnp.transpose(mod[B:], (1, 0, 2)), Ts, axis=1)

        def p_spec(chunk, width=None):
            def make(tm, tn=D):
                per_batch = T // tm
                return pl.BlockSpec((None, 1, tn), lambda r, *jk: ((r // per_batch) * ADA_CHUNKS + chunk, 0,
                                                                    jk[0] if jk else 0))
            return make

        def s_spec(chunk):
            def make(tm, tn=D):
                return pl.BlockSpec((None, tm, tn), lambda r, *jk: (chunk, r, jk[0] if jk else 0))
            return make

        groups = (("p", xp, mod_p, p_spec), ("s", xs, mod_s, s_spec))
        new_x = {}
        for name, x, md, spec in groups:
            h = _prenorm(x, md, spec(0), spec(1))
            if l % 2 == 0:
                z = _matmul(h, w_in_e, i, out_dtype=F32)
                tabs = tabs_p if name == "p" else tabs_s
                tm = tm_p if name == "p" else tm_s
                qlbn, ka, ckv, kr = _prep_even(z, tabs[0], tabs[1], g_ka[i], g_qlb[i], g_ckv[i], g_kr[i], tm)
                qh = _matmul(qlbn, w_qup, i, out_dtype=F32)
                lam_init = 0.8 - 0.6 * math.exp(-0.3 * l)
                if name == "p":
                    pa = [(t, "tab") for t in tabs_p[0]] + [(g_qa[i], "full"), (lam_rows[i], "full"), (g_sub[i], "full")]
                    o_a = _attn_prompt("a", z, 0, 2048, ka, 0, LANES, z, EVEN_VA // LANES, LANES, pa, B, T, lam_init)
                    pb = [(t, "tab") for t in tabs_p[1]] + [(g_bq[i], "full"), (w_uk_t[i], "full"), (w_uv[i], "full")]
                    o_b = _attn_prompt("b", qh, 0, HEADS * B_HEAD_SLAB, ckv, 0, B_KV_LORA, kr, 0, LANES, pb, B, T)
                    va = z[:, EVEN_VA:EVEN_VA + LANES]
                else:
                    z3 = z.reshape(Bs, Ts, EVEN_WIDTH)
                    pa = [(t, "full") for t in tabs_s8[0]] + [(g_qa[i], "full"), (lam_rows[i], "full"), (g_sub[i], "full")]
                    o_a = _attn_sample("a", z3, 0, 2048, ka.reshape(Bs, Ts, LANES), 0, LANES, z3, EVEN_VA // LANES,
                                       LANES, cache_a_k, cache_a_v, i, page_table, pa, SAMPLE_PAGES_PER_STEP, lam_init)
                    pb = [(t, "full") for t in tabs_s8[1]] + [(g_bq[i], "full"), (w_uk_t[i], "full"), (w_uv[i], "full")]
                    o_b = _attn_sample("b", qh.reshape(Bs, Ts, -1), 0, HEADS * B_HEAD_SLAB,
                                       ckv.reshape(Bs, Ts, B_KV_LORA), 0, B_KV_LORA, kr.reshape(Bs, Ts, LANES), 0,
                                       LANES, cache_b_ckv, kr_pool_t, i, page_table, pb, SAMPLE_PAGES_PER_STEP)
                    o_a = o_a.reshape(Ms, -1).astype(BF16)
                    o_b = o_b.reshape(Ms, -1).astype(BF16)
                    va = z[:, EVEN_VA:EVEN_VA + LANES]
                o = jnp.concatenate([o_a, o_b], axis=1)
                w_out = w_out_e
                outs = (("a_k", ka), ("a_v", va), ("b_ckv", ckv), ("b_kr", kr[:, :B_ROPE]))
            else:
                z = _matmul(h, w_in_o, i, out_dtype=F32)
                tabs = tabs_p if name == "p" else tabs_s
                tm = tm_p if name == "p" else tm_s
                kc = _prep_odd(z, tabs[2], g_kc[i], tm)
                vc = z[:, ODD_VC:ODD_VC + LANES]
                kd = z[:, ODD_KD:ODD_KD + LANES]
                vd = z[:, ODD_VD:ODD_VD + LANES]
                if name == "p":
                    kmean = _kmean_prompt(kc, B, T)
                    kmean = jnp.pad(kmean, ((0, 0), (0, (-kmean.shape[1]) % 8), (0, 0)))
                    pc = [(t, "tab") for t in tabs_p[2]] + [(g_qc[i], "full"), (kmean, "batch")]
                    o_c = _attn_prompt("c", z, 0, 2048, kc, 0, LANES, z, ODD_VC // LANES, LANES, pc, B, T)
                    o_d = _attn_prompt("d", z, 1, 2048, z, ODD_KD // LANES, LANES, z, ODD_VD // LANES, LANES,
                                       [(tri, "full")], B, T)
                else:
                    z3 = z.reshape(Bs, Ts, ODD_WIDTH)
                    kmean = _kmean_sample(cache_c_k, i, page_table, min(16, n_pages))
                    kmean = jnp.pad(kmean, ((0, 0), (0, (-kmean.shape[1]) % 8), (0, 0)))
                    pc = [(t, "full") for t in tabs_s8[2]] + [(g_qc[i], "full"), (kmean, "batch")]
                    o_c = _attn_sample("c", z3, 0, 2048, kc.reshape(Bs, Ts, LANES), 0, LANES, z3, ODD_VC // LANES,
                                       LANES, cache_c_k, cache_c_v, i, page_table, pc, SAMPLE_PAGES_PER_STEP)
                    o_d = _attn_sample("d", z3, 1, 2048, z3, ODD_KD // LANES, LANES, z3, ODD_VD // LANES, LANES,
                                       cache_d_k, cache_d_v, i, page_table, [(tri, "full")], SAMPLE_PAGES_PER_STEP)
                    o_c = o_c.reshape(Ms, -1).astype(BF16)
                    o_d = o_d.reshape(Ms, -1).astype(BF16)
                o = jnp.concatenate([o_c, o_d], axis=1)
                w_out = w_out_o
                outs = (("c_k", kc), ("c_v", vc), ("d_k", kd), ("d_v", vd))
            for n, r in outs:
                new_rows[n][0 if name == "p" else 1].append(r)
            x = _matmul(o, w_out, i, out_dtype=F32, tn=512, res=x, gate=md, gate_spec=spec(2))
            h2 = _prenorm(x, md, spec(3), spec(4))
            u = _matmul(h2, w_up, l, out_dtype=BF16, relu2=True)
            x = _matmul(u, w_down, l, out_dtype=F32, tk=2048, res=x, gate=md, gate_spec=spec(5))
            new_x[name] = x
        xp, xs = new_x["p"], new_x["s"]

    def stack(n):
        rp, rs = new_rows[n]
        return (jnp.stack(rp).reshape(len(rp), B, T, -1), jnp.stack(rs).reshape(len(rs), Bs, Ts, -1))

    out = [xp.reshape(B, T, D), xs.reshape(Bs, Ts, D)]
    for n in ("a_k", "a_v", "b_ckv", "b_kr", "c_k", "c_v", "d_k", "d_v"):
        out.extend(stack(n))
    return tuple(out)
```

```python
import functools
import math

import numpy as np
import jax
import jax.numpy as jnp
from jax import lax
from jax.experimental import pallas as pl
from jax.experimental.pallas import tpu as pltpu

F32 = jnp.float32
BF16 = jnp.bfloat16

EPS = 1e-6
ROPE_THETA = 500000.0
ADA_CHUNKS = 6
PAGE_SIZE = 128
MASKED_SCORE = -1e30
ALL_KEYS = 2 ** 30

LANES = 128
HEADS = 16
A_DIM = 64
A_ROT = 16
B_NOPE = 128
B_ROPE = 64
B_QK = B_NOPE + B_ROPE
B_HEAD_SLAB = 256
B_Q_LORA = 1024
B_KV_LORA = 256
B_Q_WIDTH = B_KV_LORA + LANES
C_DIM = 128
C_ROT = 32
D_DIM = 128
MOBA_BLOCK = 256
MOBA_TOPK = 3
MOBA_SHIFT = 8
KEY_BLOCK = 256
ROW_TILE = {"a": 128, "b": 256, "c": 128, "d": 128}
TILES_PER_TRIP = {"a": 64, "b": 4, "c": 16, "d": 16}
MAX_ROW_TILE = 256
SAMPLE_PAGES_PER_STEP = 16
SUFFIX_BLOCK = 128
LOG2_E = 1.4426950408889634
VMEM_LIMIT_BYTES = 56 * 1024 * 1024

EVEN_QA, EVEN_QLB, EVEN_KA, EVEN_VA, EVEN_CKV, EVEN_KR, EVEN_WIDTH = 0, 2048, 3072, 3200, 3328, 3584, 3840
ODD_QC, ODD_QD, ODD_KC, ODD_VC, ODD_KD, ODD_VD, ODD_WIDTH = 0, 2048, 4096, 4224, 4352, 4480, 4608


def _cparams(sem):
    return pltpu.CompilerParams(dimension_semantics=sem, vmem_limit_bytes=VMEM_LIMIT_BYTES)


def _dot(a, b):
    return jnp.dot(a, b, preferred_element_type=F32)


def _dot_nt(a, b):
    return lax.dot_general(a, b, (((1,), (1,)), ((), ())), preferred_element_type=F32)


def _split_bf16(x):
    hi = x.astype(BF16)
    lo = (x - hi.astype(F32)).astype(BF16)
    return hi, lo


def _mm_kernel(*refs, nk, silu, has_bias, relu2, has_res, has_x2):
    it = iter(refs)
    x_ref = next(it)
    x2_ref = next(it) if has_x2 else None
    w_ref = next(it)
    b_ref = next(it) if has_bias else None
    r_ref = next(it) if has_res else None
    g_ref = next(it) if has_res else None
    o_ref = next(it)
    acc_ref = next(it) if nk > 1 else None

    x = x_ref[...]
    if silu:
        x = (x * (1.0 / (1.0 + jnp.exp(-x)))).astype(BF16)
    if has_x2:
        k1 = x.shape[1]
        part = _dot(x, w_ref[:k1, :]) + _dot(x2_ref[...], w_ref[k1:, :])
    else:
        part = _dot(x, w_ref[...].astype(BF16))

    def epilogue(acc):
        if has_bias:
            acc = acc + b_ref[...]
        if relu2:
            acc = jnp.maximum(acc, 0.0)
            acc = acc * acc
        if has_res:
            acc = r_ref[...] + g_ref[...] * acc
        o_ref[...] = acc.astype(o_ref.dtype)

    if nk == 1:
        epilogue(part)
    else:
        k = pl.program_id(2)

        @pl.when(k == 0)
        def _():
            acc_ref[...] = part

        @pl.when(k > 0)
        def _():
            acc_ref[...] += part

        @pl.when(k == nk - 1)
        def _():
            epilogue(acc_ref[...])


def _pick_tile(n, pref, unit=LANES):
    if n <= pref:
        return n
    t = pref - pref % unit
    while n % t:
        t -= unit
    return t


def _matmul(x, w, layer, *, out_dtype, tm=1024, tn=1024, tk=4096, silu=False, bias=None, relu2=False,
            res=None, gate=None, gate_spec=None, x2=None):
    M, K = x.shape
    N = w.shape[-1]
    if x2 is not None:
        K = tk = K + x2.shape[1]
    tm, tn, tk = _pick_tile(M, tm), _pick_tile(N, tn), _pick_tile(K, tk)
    nk = K // tk
    in_specs = [pl.BlockSpec((tm, x.shape[1] if x2 is not None else tk), lambda i, j, k: (i, k))]
    args = [x]
    if x2 is not None:
        in_specs.append(pl.BlockSpec((tm, x2.shape[1]), lambda i, j, k: (i, 0)))
        args.append(x2)
    in_specs.append(pl.BlockSpec((None, tk, tn), lambda i, j, k: (layer, k, j)))
    args.append(w)
    if bias is not None:
        in_specs.append(pl.BlockSpec((1, tn), lambda i, j, k: (0, j)))
        args.append(bias)
    if res is not None:
        in_specs.append(pl.BlockSpec((tm, tn), lambda i, j, k: (i, j)))
        in_specs.append(gate_spec(tm, tn))
        args += [res, gate]
    scratch = [pltpu.VMEM((tm, tn), F32)] if nk > 1 else []
    kern = functools.partial(_mm_kernel, nk=nk, silu=silu, has_bias=bias is not None, relu2=relu2,
                             has_res=res is not None, has_x2=x2 is not None)
    return pl.pallas_call(
        kern, grid=(M // tm, N // tn, nk), in_specs=in_specs,
        out_specs=pl.BlockSpec((tm, tn), lambda i, j, k: (i, j)),
        out_shape=jax.ShapeDtypeStruct((M, N), out_dtype), scratch_shapes=scratch,
        compiler_params=_cparams(("parallel", "parallel", "arbitrary")), name="matmul")(*args)


def _prenorm_kernel(x_ref, shift_ref, scale_ref, o_ref):
    x = x_ref[...]
    y = x * lax.rsqrt(jnp.mean(x * x, axis=-1, keepdims=True) + EPS)
    o_ref[...] = (y * (1.0 + scale_ref[...]) + shift_ref[...]).astype(o_ref.dtype)


def _prenorm(x, mod, shift_spec, scale_spec, tm=256):
    M, D = x.shape
    tm = _pick_tile(M, tm)
    return pl.pallas_call(
        _prenorm_kernel, grid=(M // tm,),
        in_specs=[pl.BlockSpec((tm, D), lambda i: (i, 0)), shift_spec(tm), scale_spec(tm)],
        out_specs=pl.BlockSpec((tm, D), lambda i: (i, 0)),
        out_shape=jax.ShapeDtypeStruct((M, D), BF16),
        compiler_params=_cparams(("parallel",)), name="prenorm")(x, mod, mod)


def _lane_iota(shape):
    return lax.broadcasted_iota(jnp.int32, shape, len(shape) - 1)


def _rope(y, tab, half):
    c, s_up, s_dn = tab
    return y * c + pltpu.roll(y, LANES - half, 1) * s_up + pltpu.roll(y, half, 1) * s_dn


def _norm_pair64(x, gain):
    lo = _lane_iota(x.shape) < A_DIM
    sq = x * x
    s_lo = jnp.sum(jnp.where(lo, sq, 0.0), axis=-1, keepdims=True)
    s_hi = jnp.sum(jnp.where(lo, 0.0, sq), axis=-1, keepdims=True)
    inv = jnp.where(lo, lax.rsqrt(s_lo * (1.0 / A_DIM) + EPS), lax.rsqrt(s_hi * (1.0 / A_DIM) + EPS))
    return x * inv * gain


def _norm_rows(x, gain, n_real):
    ms = jnp.sum(x * x, axis=-1, keepdims=True) * (1.0 / n_real)
    return x * lax.rsqrt(ms + EPS) * gain


def _rope_tables(pos, rot, width, period):
    half = rot // 2
    inv = ROPE_THETA ** (-jnp.arange(half, dtype=F32) * 2.0 / rot)
    ang = pos.astype(F32)[:, None] * inv[None, :]
    cos, sin = jnp.cos(ang), jnp.sin(ang)
    lane = np.arange(LANES)
    g = lane % period
    first = (g < half) & (lane < width)
    second = (g >= half) & (g < rot) & (lane < width)
    idx = np.where(first, g, np.where(second, g - half, 0))
    cos_l, sin_l = cos[:, idx], sin[:, idx]
    rot_l = jnp.asarray(first | second)
    c = jnp.where(rot_l[None, :], cos_l, 1.0)
    s_up = jnp.where(jnp.asarray(first)[None, :], -sin_l, 0.0)
    s_dn = jnp.where(jnp.asarray(second)[None, :], sin_l, 0.0)
    return c, s_up, s_dn


def _prep_even_kernel(qlb_ref, ka_ref, ckv_ref, kr_ref, ca, sua, sda, cr, sur, sdr,
                      g_ka, g_qlb, g_ckv, g_kr, qlbn_o, ka_o, ckv_o, kr_o):
    qlb = qlb_ref[...]
    qlbn_o[...] = _norm_rows(qlb, g_qlb[...], B_Q_LORA).astype(qlbn_o.dtype)
    ka = _norm_pair64(ka_ref[...], g_ka[...])
    ka_o[...] = _rope(ka, (ca[...], sua[...], sda[...]), A_ROT // 2)
    ckv_o[...] = _norm_rows(ckv_ref[...], g_ckv[...], B_KV_LORA)
    kr = _norm_rows(kr_ref[...], g_kr[...], B_ROPE)
    kr = _rope(kr, (cr[...], sur[...], sdr[...]), B_ROPE // 2)
    kr_o[...] = kr


def _prep_odd_kernel(kc_ref, cc, suc, sdc, g_kc, kc_o):
    kc = _norm_rows(kc_ref[...], g_kc[...], C_DIM)
    kc_o[...] = _rope(kc, (cc[...], suc[...], sdc[...]), C_ROT // 2)


def _table_specs(tm, n_tab_blocks):
    return [pl.BlockSpec((tm, LANES), lambda i: (i % n_tab_blocks, 0))] * 3


def _full(shape):
    return pl.BlockSpec(shape, lambda *_: (0,) * len(shape))


def _prep_even(z, tab_a, tab_r, g_ka, g_qlb, g_ckv, g_kr, tm):
    M = z.shape[0]
    nt = tab_a[0].shape[0] // tm
    col = lambda width, start: pl.BlockSpec((tm, width), lambda i: (i, start // width))
    row = lambda width: pl.BlockSpec((tm, width), lambda i: (i, 0))
    return pl.pallas_call(
        _prep_even_kernel, grid=(M // tm,),
        in_specs=[col(B_Q_LORA, EVEN_QLB), col(LANES, EVEN_KA), col(B_KV_LORA, EVEN_CKV), col(LANES, EVEN_KR)]
        + _table_specs(tm, nt) + _table_specs(tm, nt)
        + [_full((1, LANES)), _full((1, B_Q_LORA)), _full((1, B_KV_LORA)), _full((1, LANES))],
        out_specs=[row(B_Q_LORA), row(LANES), row(B_KV_LORA), row(LANES)],
        out_shape=[jax.ShapeDtypeStruct((M, B_Q_LORA), BF16), jax.ShapeDtypeStruct((M, LANES), F32),
                   jax.ShapeDtypeStruct((M, B_KV_LORA), F32), jax.ShapeDtypeStruct((M, LANES), F32)],
        compiler_params=_cparams(("parallel",)), name="prep_even")(
            z, z, z, z, *tab_a, *tab_r, g_ka, g_qlb, g_ckv, g_kr)


def _prep_odd(z, tab_c, g_kc, tm):
    M = z.shape[0]
    nt = tab_c[0].shape[0] // tm
    return pl.pallas_call(
        _prep_odd_kernel, grid=(M // tm,),
        in_specs=[pl.BlockSpec((tm, LANES), lambda i: (i, ODD_KC // LANES))] + _table_specs(tm, nt)
        + [_full((1, LANES))],
        out_specs=pl.BlockSpec((tm, LANES), lambda i: (i, 0)),
        out_shape=jax.ShapeDtypeStruct((M, LANES), F32),
        compiler_params=_cparams(("parallel",)), name="prep_odd")(z, *tab_c, g_kc)


def _kmean_kernel(k_ref, o_ref):
    rows = k_ref[...]
    nb = rows.shape[0] // MOBA_BLOCK
    o_ref[...] = jnp.sum(rows.reshape(nb, MOBA_BLOCK, rows.shape[1]), axis=1) * (1.0 / MOBA_BLOCK)


def _page_sum_kernel(p_ref, o_ref):
    o_ref[...] = jnp.sum(p_ref[...], axis=1)


def _kmean_gather_kernel(pt_ref, sums_ref, o_ref, *, n_blocks):
    b = pl.program_id(0)
    per_block = MOBA_BLOCK // PAGE_SIZE
    for blk in range(n_blocks):
        acc = sums_ref[pl.ds(pt_ref[b, blk * per_block], 1), :]
        for j in range(1, per_block):
            acc = acc + sums_ref[pl.ds(pt_ref[b, blk * per_block + j], 1), :]
        o_ref[blk:blk + 1, :] = acc * (1.0 / MOBA_BLOCK)


def _kmean_prompt(kc, B, T):
    nb = T // MOBA_BLOCK
    out = pl.pallas_call(
        _kmean_kernel, grid=(B,),
        in_specs=[pl.BlockSpec((T, C_DIM), lambda b: (b, 0))],
        out_specs=pl.BlockSpec((None, nb, C_DIM), lambda b: (b, 0, 0)),
        out_shape=jax.ShapeDtypeStruct((B, nb, C_DIM), F32),
        compiler_params=_cparams(("parallel",)), name="kmean_prompt")(kc)
    return out


def _kmean_sample(pool, layer, page_table):
    Bs, n_pages = page_table.shape
    n_phys = pool.shape[1]
    group = _pick_tile(n_phys, 64, 8)
    sums = pl.pallas_call(
        _page_sum_kernel, grid=(n_phys // group,),
        in_specs=[pl.BlockSpec((None, group, PAGE_SIZE, C_DIM), lambda g: (layer, g, 0, 0))],
        out_specs=pl.BlockSpec((group, C_DIM), lambda g: (g, 0)),
        out_shape=jax.ShapeDtypeStruct((n_phys, C_DIM), F32),
        compiler_params=_cparams(("parallel",)), name="page_sums")(pool)
    n_blocks = n_pages * PAGE_SIZE // MOBA_BLOCK
    gs = pltpu.PrefetchScalarGridSpec(
        num_scalar_prefetch=1, grid=(Bs,),
        in_specs=[pl.BlockSpec((n_phys, C_DIM), lambda b, pt: (0, 0))],
        out_specs=pl.BlockSpec((None, n_blocks, C_DIM), lambda b, pt: (b, 0, 0)))
    return pl.pallas_call(
        functools.partial(_kmean_gather_kernel, n_blocks=n_blocks), grid_spec=gs,
        out_shape=jax.ShapeDtypeStruct((Bs, n_blocks, C_DIM), F32),
        compiler_params=_cparams(("parallel",)), name="kmean_sample")(page_table, sums)


def _moba_select(q_t, kmean, own):
    q_hi, q_lo = _split_bf16(q_t)
    k_hi, k_lo = _split_bf16(kmean)
    gate = _dot(k_hi, q_hi) + _dot(k_lo, q_hi) + _dot(k_hi, q_lo)
    slot = lax.broadcasted_iota(jnp.int32, gate.shape, 0)
    slot_f = slot.astype(F32)
    valid = slot < own
    gate = jnp.where(valid, gate, -jnp.inf)
    sel = jnp.zeros(gate.shape, F32)
    for _ in range(MOBA_TOPK):
        best = jnp.max(gate, axis=0, keepdims=True)
        first = jnp.min(jnp.where(gate == best, slot_f, float(gate.shape[0])), axis=0, keepdims=True)
        pick = slot_f == first
        sel = jnp.where(pick & valid, 1.0, sel)
        gate = jnp.where(pick, -jnp.inf, gate)
    return sel


def _online_softmax(s_t, v_t, state):
    m_old, l_old, acc_old = state
    m_new = jnp.maximum(m_old, jnp.max(s_t, axis=0, keepdims=True))
    alpha = jnp.exp2(m_old - m_new)
    p = jnp.exp2(s_t - m_new)
    l_new = alpha * l_old + jnp.sum(p, axis=0, keepdims=True)
    return m_new, l_new, alpha * acc_old + _dot(v_t, p.astype(BF16))


def _suffix_sums(x, tri):
    n, w = x.shape
    pieces, later = [], None
    for b in reversed(range(n // SUFFIX_BLOCK)):
        xb = x[b * SUFFIX_BLOCK:(b + 1) * SUFFIX_BLOCK]
        hi, lo = _split_bf16(xb)
        both = _dot(tri, jnp.concatenate([hi, lo], axis=1))
        s = both[:, :w] + both[:, w:]
        total = s[0:1] + xb[0:1]
        if later is not None:
            s, total = s + later, total + later
        pieces.insert(0, s)
        later = total
    return jnp.concatenate(pieces, axis=0), later


def _attn_tiling(variant, M):
    rows = M * (2 if variant == "a" else 1)
    rt = min(ROW_TILE[variant], rows)
    return rt, rows // rt


def _attn_kernel(*refs, variant, tq, sample, pages, n_chunks, past_len, lam_init):
    M = HEADS * tq
    rt, n_tiles = _attn_tiling(variant, M)
    it = iter(refs)
    if sample:
        next(it)
    q_ref = next(it)
    k_ref = next(it)
    v_ref = next(it)
    if sample:
        kp_refs = [next(it) for _ in range(pages)]
        vp_refs = [next(it) for _ in range(pages)]
    tab = None
    if variant in "abc":
        tab = (next(it), next(it), next(it))
    if variant == "a":
        gq_ref, lam_ref, gsub_ref = next(it), next(it), next(it)
    elif variant == "b":
        gq_ref, wuk_ref, wuv_ref = next(it), next(it), next(it)
    elif variant == "c":
        gq_ref, kmean_ref = next(it), next(it)
    else:
        tri_ref = next(it)
    o_ref = next(it)
    q_scr = next(it)
    m1, l1, acc1 = next(it), next(it), next(it)
    if variant == "c":
        sel_scr = next(it)
    state_refs = (m1, acc1) if variant == "d" else (m1, l1, acc1)

    if sample:
        step = pl.program_id(1)
        q_base = past_len
        last_step = n_chunks - 1
    else:
        qi = pl.program_id(1)
        step = pl.program_id(2)
        q_base = qi * tq
        last_step = n_chunks - 1

    q_scale = {"a": A_DIM ** -0.5 * LOG2_E, "b": B_QK ** -0.5 * LOG2_E, "c": C_DIM ** -0.5 * LOG2_E,
               "d": D_DIM ** -0.5 * LOG2_E}[variant]

    @pl.when(step == 0)
    def _init():
        t = None if tab is None else tuple(r[...] for r in tab)
        pieces, second = [], []
        for h in range(HEADS):
            if variant == "a":
                y = _norm_pair64(q_ref[:, h * LANES:(h + 1) * LANES], gq_ref[...])
                y = _rope(y, t, A_ROT // 2) * q_scale
                lo = _lane_iota(y.shape) < A_DIM
                pieces.append(jnp.where(lo, y, 0.0))
                second.append(jnp.where(lo, 0.0, y))
            elif variant == "b":
                x = q_ref[:, h * B_HEAD_SLAB:(h + 1) * B_HEAD_SLAB]
                y = _norm_rows(x, gq_ref[...], B_QK)
                lat = _dot(y[:, :B_NOPE].astype(BF16), wuk_ref[h])
                rp = _rope(y[:, B_NOPE:], t, B_ROPE // 2)
                pieces.append(jnp.concatenate([lat, rp], axis=1) * q_scale)
            elif variant == "c":
                y = _norm_rows(q_ref[:, h * LANES:(h + 1) * LANES], gq_ref[...], C_DIM)
                y = _rope(y, t, C_ROT // 2)
                pieces.append(y)
            else:
                pieces.append(q_ref[:, h * LANES:(h + 1) * LANES] * q_scale)
        stacked = jnp.concatenate(pieces + second, axis=0)
        for r in range(n_tiles):
            x_t = stacked[r * rt:(r + 1) * rt].T
            if variant == "c":
                row = r * rt + lax.broadcasted_iota(jnp.int32, (1, rt), 1)
                own = lax.shift_right_arithmetic(q_base + (row & (tq - 1)), MOBA_SHIFT)
                sel_scr[r] = _moba_select(x_t, kmean_ref[...], own)
                x_t = x_t * q_scale
            q_scr[r] = x_t.astype(BF16)
        if variant == "d":
            m1[...] = jnp.zeros(m1.shape, F32)
        else:
            m1[...] = jnp.full(m1.shape, MASKED_SCORE, F32)
            l1[...] = jnp.zeros(l1.shape, F32)
        acc1[...] = jnp.zeros(acc1.shape, F32)

    def block(k, v, k_base, causal, v_is_t=False):
        n = k.shape[0]
        if variant == "b":
            kb = k.astype(BF16)
            v_t = k.T.astype(BF16)
            if v_is_t:
                v = jnp.concatenate([v, jnp.zeros((LANES - B_ROPE, n), F32)], axis=0).T
            kb = jnp.concatenate([kb, v.astype(BF16)], axis=1)
        else:
            kb, v_t = k.astype(BF16), v.T.astype(BF16)
        if variant == "c":
            slot = k_base // MOBA_BLOCK
        def tile(r, state):
            q = q_scr[r]
            row = r * rt + lax.broadcasted_iota(jnp.int32, (1, rt), 1)
            qpos = q_base + (row & (tq - 1))
            kpos = k_base + lax.broadcasted_iota(jnp.int32, (n, rt), 0)
            if variant == "a":
                s_t = _dot(kb, q)
                if causal:
                    s_t = jnp.where(kpos <= qpos, s_t, MASKED_SCORE)
                return _online_softmax(s_t, v_t, state)
            if variant == "b":
                s_t = _dot(kb, q)
                if causal:
                    s_t = jnp.where(kpos <= qpos, s_t, MASKED_SCORE)
                return _online_softmax(s_t, v_t, state)
            if variant == "c":
                own = lax.shift_right_arithmetic(qpos, MOBA_SHIFT)
                sel = sel_scr[r]
                slot_iota = lax.broadcasted_iota(jnp.int32, sel.shape, 0)
                span = min(n, MOBA_BLOCK)
                allowed = []
                for b in range(n // span):
                    hit = jnp.sum(jnp.where(slot_iota == slot + b, sel, 0.0), axis=0, keepdims=True)
                    limit = jnp.where(own == slot + b, qpos, jnp.where(hit > 0.5, ALL_KEYS, -1))
                    allowed.append(kpos[b * span:(b + 1) * span] <= limit)
                allowed = allowed[0] if len(allowed) == 1 else jnp.concatenate(allowed, axis=0)
                s_t = jnp.where(allowed, _dot(kb, q), MASKED_SCORE)
                return _online_softmax(s_t, v_t, state)
            carry, acc = state
            z = _dot(kb, q)
            log_beta = jnp.minimum(z, 0.0) - jnp.log(1.0 + jnp.exp2(-jnp.abs(z))) * LOG2_E
            log_1m = log_beta - z
            if causal:
                strict = kpos < qpos
                log_1m = jnp.where(strict, log_1m, 0.0)
            newer, total = _suffix_sums(log_1m, tri_ref[...])
            a = jnp.exp2(log_beta + (carry + newer))
            if causal:
                a = jnp.where(strict, a, 0.0)
            return carry + total, acc + _dot(v_t, a.astype(BF16))

        group = min(n_tiles, TILES_PER_TRIP[variant])

        def trip(t):
            rs = [t * group + g for g in range(group)]
            states = [tuple(ref[r] for ref in state_refs) for r in rs]
            for r, new in zip(rs, [tile(r, st) for r, st in zip(rs, states)]):
                for ref, val in zip(state_refs, new):
                    ref[r] = val

        if n_tiles == group:
            trip(0)
        else:
            def body(t, carry):
                trip(t)
                return carry
            lax.fori_loop(0, n_tiles // group, body, 0)

    if sample:
        @pl.when(step == 0)
        def _new_rows():
            pad = PAGE_SIZE - tq
            k = jnp.concatenate([k_ref[...], jnp.zeros((pad, k_ref.shape[1]), F32)], axis=0)
            v = jnp.concatenate([v_ref[...], jnp.zeros((pad, v_ref.shape[1]), F32)], axis=0)
            block(k, v, past_len, True)

        ch = n_chunks - 1 - step
        v_is_t = variant == "b"
        k = jnp.concatenate([r[...] for r in kp_refs], axis=0)
        v = jnp.concatenate([r[...] for r in vp_refs], axis=1 if v_is_t else 0)
        block(k, v, ch * (pages * PAGE_SIZE), False, v_is_t)
    else:
        @pl.when(step == 0)
        def _diag():
            block(k_ref[...], v_ref[...], qi * KEY_BLOCK, True)

        @pl.when((step > 0) & (step <= qi))
        def _past():
            block(k_ref[...], v_ref[...], (qi - step) * KEY_BLOCK, False)

    @pl.when(step == last_step)
    def _fin():
        if variant == "a":
            lp = lam_ref[...]
            lam = (jnp.exp(jnp.sum(lp[0:1] * lp[1:2], axis=-1, keepdims=True))
                   - jnp.exp(jnp.sum(lp[2:3] * lp[3:4], axis=-1, keepdims=True)) + lam_init)
        n_out = M // min(rt, M)
        w = min(rt, M)
        for r in range(n_out):
            if variant == "a":
                if n_tiles == 1:
                    both = acc1[0] / l1[0]
                    o_t = both[:, :M] - lam * both[:, M:]
                else:
                    r2 = r + n_out
                    o_t = acc1[r] / l1[r] - lam * (acc1[r2] / l1[r2])
                o_t = (o_t * lax.rsqrt(jnp.mean(o_t * o_t, axis=0, keepdims=True) + EPS)
                       * gsub_ref[:, :w] * (1.0 - lam_init))
            elif variant == "d":
                o_t = acc1[r]
            else:
                o_t = acc1[r] / l1[r]
            o = o_t.T
            for u in range(w // min(tq, w)):
                rows = min(tq, w)
                h, j = divmod(r * w + u * rows, tq)
                oh = o[u * rows:(u + 1) * rows]
                if variant == "b":
                    oh = _dot(oh.astype(BF16), wuv_ref[h])
                o_ref[j:j + rows, h * LANES:(h + 1) * LANES] = oh.astype(o_ref.dtype)


def _attn_scratch(variant, M, params):
    rt, n_tiles = _attn_tiling(variant, M)
    dq = B_Q_WIDTH if variant == "b" else LANES
    dv = B_KV_LORA if variant == "b" else LANES
    scr = [pltpu.VMEM((n_tiles, dq, rt), BF16), pltpu.VMEM((n_tiles, 1, rt), F32), pltpu.VMEM((n_tiles, 1, rt), F32),
           pltpu.VMEM((n_tiles, dv, rt), F32)]
    if variant == "c":
        n_slots = next(a.shape[1] for a, kind in params if kind == "batch")
        scr.append(pltpu.VMEM((n_tiles, n_slots, rt), F32))
    return scr


def _attn_prompt(variant, q_arr, q_col, q_width, k_arr, k_col, k_width, v_arr, v_col, v_width, params, B, T,
                 lam_init=0.0):
    tq = min(KEY_BLOCK, T)
    nq = T // tq
    M = HEADS * tq

    def kv_spec(col, width):
        return pl.BlockSpec((tq, width), lambda b, qi, ki: (b * nq + jnp.maximum(qi - ki, 0), col))

    in_specs = [pl.BlockSpec((tq, q_width), lambda b, qi, ki: (b * nq + qi, q_col)),
                kv_spec(k_col, k_width), kv_spec(v_col, v_width)]
    args = [q_arr, k_arr, v_arr]
    for arr, kind in params:
        if kind == "tab":
            in_specs.append(pl.BlockSpec((tq, LANES), lambda b, qi, ki: (qi, 0)))
        elif kind == "batch":
            in_specs.append(pl.BlockSpec((None,) + arr.shape[1:], lambda b, qi, ki: (b,) + (0,) * (arr.ndim - 1)))
        else:
            in_specs.append(_full(arr.shape))
        args.append(arr)
    kern = functools.partial(_attn_kernel, variant=variant, tq=tq, sample=False, pages=0, n_chunks=nq,
                             past_len=0, lam_init=lam_init)
    return pl.pallas_call(
        kern, grid=(B, nq, nq), in_specs=in_specs,
        out_specs=pl.BlockSpec((tq, HEADS * LANES), lambda b, qi, ki: (b * nq + qi, 0)),
        out_shape=jax.ShapeDtypeStruct((B * T, HEADS * LANES), BF16),
        scratch_shapes=_attn_scratch(variant, M, params),
        compiler_params=_cparams(("parallel", "parallel", "arbitrary")), name="attn_prompt_" + variant)(*args)


def _attn_sample(variant, q_arr, q_col, q_width, k_arr, k_col, k_width, v_arr, v_col, v_width,
                 k_pool, v_pool, layer, page_table, params, pages, lam_init=0.0):
    Bs, Ts = q_arr.shape[:2]
    n_pages = page_table.shape[1]
    pages = min(pages, n_pages)
    n_chunks = n_pages // pages
    M = HEADS * Ts

    def new_spec(col, width):
        return pl.BlockSpec((None, Ts, width), lambda b, c, pt: (b, 0, col))

    def page_spec(pool, j):
        return pl.BlockSpec((None, None) + pool.shape[2:],
                            lambda b, c, pt: (layer, pt[b, (n_chunks - 1 - c) * pages + j], 0, 0))

    in_specs = [new_spec(q_col, q_width), new_spec(k_col, k_width), new_spec(v_col, v_width)]
    in_specs += [page_spec(k_pool, j) for j in range(pages)]
    in_specs += [page_spec(v_pool, j) for j in range(pages)]
    args = [q_arr, k_arr, v_arr] + [k_pool] * pages + [v_pool] * pages
    for arr, kind in params:
        if kind == "batch":
            in_specs.append(pl.BlockSpec((None,) + arr.shape[1:], lambda b, c, pt: (b,) + (0,) * (arr.ndim - 1)))
        else:
            in_specs.append(_full(arr.shape))
        args.append(arr)
    gs = pltpu.PrefetchScalarGridSpec(
        num_scalar_prefetch=1, grid=(Bs, n_chunks), in_specs=in_specs,
        out_specs=pl.BlockSpec((None, Ts, HEADS * LANES), lambda b, c, pt: (b, 0, 0)),
        scratch_shapes=_attn_scratch(variant, M, params))
    kern = functools.partial(_attn_kernel, variant=variant, tq=Ts, sample=True, pages=pages, n_chunks=n_chunks,
                             past_len=n_pages * PAGE_SIZE, lam_init=lam_init)
    return pl.pallas_call(
        kern, grid_spec=gs, out_shape=jax.ShapeDtypeStruct((Bs, Ts, HEADS * LANES), F32),
        compiler_params=_cparams(("parallel", "arbitrary")), name="attn_sample_" + variant)(page_table, *args)


def _pad_last(a, n):
    return jnp.pad(a, [(0, 0)] * (a.ndim - 1) + [(0, n - a.shape[-1])])


def kernel(x_prompt, x_sample, cache_a_k, cache_a_v, cache_b_ckv, cache_b_kr, cache_c_k, cache_c_v, cache_d_k, cache_d_v, page_table, c_prompt, c_sample, w_ada, b_ada, ada_table, w_in_even, a_q_norm, a_k_norm, a_lambda_q1, a_lambda_k1, a_lambda_q2, a_lambda_k2, a_sub_norm, b_q_lat_norm, b_w_q_up, b_q_norm, b_kv_norm, b_kr_norm, b_w_uk, b_w_uv, w_out_even, w_in_odd, c_q_norm, c_k_norm, w_out_odd, w_mlp_up, w_mlp_down):
    B, T, D = x_prompt.shape
    Bs, Ts, _ = x_sample.shape
    depth = ada_table.shape[0]
    n_pages = page_table.shape[1]
    past_len = n_pages * PAGE_SIZE
    Mp, Ms = B * T, Bs * Ts
    n_even = w_in_even.shape[0]
    n_odd = w_in_odd.shape[0]

    wie = w_in_even
    w_in_e = jnp.concatenate(
        [wie[..., 0:2048], wie[..., 2304:3328], wie[..., 2048:2304], wie[..., 3328:3648],
         jnp.zeros(wie.shape[:2] + (EVEN_WIDTH - 3648,), wie.dtype)], axis=-1).astype(BF16)
    wio = w_in_odd
    w_in_o = jnp.concatenate([wio[..., 0:2048], wio[..., 2304:4352], wio[..., 2048:2304], wio[..., 4352:4608]],
                             axis=-1).astype(BF16)
    w_qup = _pad_last(b_w_q_up, B_HEAD_SLAB).reshape(n_even, B_Q_LORA, HEADS * B_HEAD_SLAB).astype(BF16)
    w_uk_t = jnp.transpose(b_w_uk, (0, 2, 3, 1)).astype(BF16)
    w_uv = jnp.transpose(b_w_uv, (0, 2, 1, 3)).astype(BF16)
    w_out_e = w_out_even.astype(BF16)
    w_out_o = w_out_odd.astype(BF16)
    w_up = w_mlp_up
    w_down = w_mlp_down.astype(BF16)
    w_ada_b = w_ada[None]

    g_qa = jnp.tile(a_q_norm, (1, 2))[:, None, :]
    g_ka = jnp.tile(a_k_norm, (1, 2))[:, None, :]
    g_sub = jnp.broadcast_to(a_sub_norm[:, :, None], a_sub_norm.shape + (MAX_ROW_TILE,))
    g_qlb = b_q_lat_norm[:, None, :]
    g_bq = _pad_last(b_q_norm, B_HEAD_SLAB)[:, None, :]
    g_ckv = b_kv_norm[:, None, :]
    g_kr = _pad_last(b_kr_norm, LANES)[:, None, :]
    g_qc = c_q_norm[:, None, :]
    g_kc = c_k_norm[:, None, :]
    lam_rows = jnp.stack([a_lambda_q1, a_lambda_k1, a_lambda_q2, a_lambda_k2], axis=1)
    lam_rows = jnp.pad(lam_rows, ((0, 0), (0, 4), (0, LANES - A_DIM)))
    tri = jnp.asarray(np.triu(np.ones((SUFFIX_BLOCK, SUFFIX_BLOCK), np.float32), 1), BF16)
    kr_pool_t = jnp.swapaxes(cache_b_kr, 2, 3)

    pos_p = jnp.arange(T)
    pos_s = past_len + jnp.arange(Ts)
    tm_p = _pick_tile(Mp, 256)
    tm_s = _pick_tile(Ms, 256)

    def tables(pos, tile_rows):
        reps = max(1, tile_rows // pos.shape[0])
        mk = lambda rot, width, period: tuple(jnp.tile(t, (reps, 1)) for t in _rope_tables(pos, rot, width, period))
        return mk(A_ROT, LANES, A_DIM), mk(B_ROPE, B_ROPE, LANES), mk(C_ROT, LANES, LANES)

    tabs_p = tables(pos_p, tm_p)
    tabs_s = tables(pos_s, tm_s)
    tabs_s8 = tables(pos_s, Ts)

    c_all = jnp.concatenate([c_prompt, c_sample], axis=0)
    n_c = c_all.shape[0]
    c_all = jnp.pad(c_all, ((0, (-n_c) % 16), (0, 0)))
    base = _matmul(c_all, w_ada_b, 0, out_dtype=F32, tm=c_all.shape[0], tn=1024, silu=True,
                   bias=b_ada[None, :])[:n_c].reshape(n_c, ADA_CHUNKS, D)

    xp = x_prompt.reshape(Mp, D)
    xs = x_sample.reshape(Ms, D)
    new_rows = {n: ([], []) for n in ("a_k", "a_v", "b_ckv", "b_kr", "c_k", "c_v", "d_k", "d_v")}

    for l in range(depth):
        i = l // 2
        mod = base + ada_table[l]
        mod_p = mod[:B].reshape(B * ADA_CHUNKS, 1, D)
        mod_s = jnp.repeat(jnp.transpose(mod[B:], (1, 0, 2)), Ts, axis=1)

        def p_spec(chunk, width=None):
            def make(tm, tn=D):
                per_batch = T // tm
                return pl.BlockSpec((None, 1, tn), lambda r, *jk: ((r // per_batch) * ADA_CHUNKS + chunk, 0,
                                                                    jk[0] if jk else 0))
            return make

        def s_spec(chunk):
            def make(tm, tn=D):
                return pl.BlockSpec((None, tm, tn), lambda r, *jk: (chunk, r, jk[0] if jk else 0))
            return make

        groups = (("p", xp, mod_p, p_spec), ("s", xs, mod_s, s_spec))
        new_x = {}
        for name, x, md, spec in groups:
            h = _prenorm(x, md, spec(0), spec(1))
            if l % 2 == 0:
                z = _matmul(h, w_in_e, i, out_dtype=F32)
                tabs = tabs_p if name == "p" else tabs_s
                tm = tm_p if name == "p" else tm_s
                qlbn, ka, ckv, kr = _prep_even(z, tabs[0], tabs[1], g_ka[i], g_qlb[i], g_ckv[i], g_kr[i], tm)
                qh = _matmul(qlbn, w_qup, i, out_dtype=F32)
                lam_init = 0.8 - 0.6 * math.exp(-0.3 * l)
                if name == "p":
                    pa = [(t, "tab") for t in tabs_p[0]] + [(g_qa[i], "full"), (lam_rows[i], "full"), (g_sub[i], "full")]
                    o_a = _attn_prompt("a", z, 0, 2048, ka, 0, LANES, z, EVEN_VA // LANES, LANES, pa, B, T, lam_init)
                    pb = [(t, "tab") for t in tabs_p[1]] + [(g_bq[i], "full"), (w_uk_t[i], "full"), (w_uv[i], "full")]
                    o_b = _attn_prompt("b", qh, 0, HEADS * B_HEAD_SLAB, ckv, 0, B_KV_LORA, kr, 0, LANES, pb, B, T)
                    va = z[:, EVEN_VA:EVEN_VA + LANES]
                else:
                    z3 = z.reshape(Bs, Ts, EVEN_WIDTH)
                    pa = [(t, "full") for t in tabs_s8[0]] + [(g_qa[i], "full"), (lam_rows[i], "full"), (g_sub[i], "full")]
                    o_a = _attn_sample("a", z3, 0, 2048, ka.reshape(Bs, Ts, LANES), 0, LANES, z3, EVEN_VA // LANES,
                                       LANES, cache_a_k, cache_a_v, i, page_table, pa, SAMPLE_PAGES_PER_STEP, lam_init)
                    pb = [(t, "full") for t in tabs_s8[1]] + [(g_bq[i], "full"), (w_uk_t[i], "full"), (w_uv[i], "full")]
                    o_b = _attn_sample("b", qh.reshape(Bs, Ts, -1), 0, HEADS * B_HEAD_SLAB,
                                       ckv.reshape(Bs, Ts, B_KV_LORA), 0, B_KV_LORA, kr.reshape(Bs, Ts, LANES), 0,
                                       LANES, cache_b_ckv, kr_pool_t, i, page_table, pb, SAMPLE_PAGES_PER_STEP)
                    o_a = o_a.reshape(Ms, -1).astype(BF16)
                    o_b = o_b.reshape(Ms, -1).astype(BF16)
                    va = z[:, EVEN_VA:EVEN_VA + LANES]
                o, o2 = o_a, o_b
                w_out = w_out_e
                outs = (("a_k", ka), ("a_v", va), ("b_ckv", ckv), ("b_kr", kr[:, :B_ROPE]))
            else:
                z = _matmul(h, w_in_o, i, out_dtype=F32)
                tabs = tabs_p if name == "p" else tabs_s
                tm = tm_p if name == "p" else tm_s
                kc = _prep_odd(z, tabs[2], g_kc[i], tm)
                vc = z[:, ODD_VC:ODD_VC + LANES]
                kd = z[:, ODD_KD:ODD_KD + LANES]
                vd = z[:, ODD_VD:ODD_VD + LANES]
                if name == "p":
                    kmean = _kmean_prompt(kc, B, T)
                    kmean = jnp.pad(kmean, ((0, 0), (0, (-kmean.shape[1]) % 8), (0, 0)))
                    pc = [(t, "tab") for t in tabs_p[2]] + [(g_qc[i], "full"), (kmean, "batch")]
                    o_c = _attn_prompt("c", z, 0, 2048, kc, 0, LANES, z, ODD_VC // LANES, LANES, pc, B, T)
                    o_d = _attn_prompt("d", z, 1, 2048, z, ODD_KD // LANES, LANES, z, ODD_VD // LANES, LANES,
                                       [(tri, "full")], B, T)
                else:
                    z3 = z.reshape(Bs, Ts, ODD_WIDTH)
                    kmean = _kmean_sample(cache_c_k, i, page_table)
                    kmean = jnp.pad(kmean, ((0, 0), (0, (-kmean.shape[1]) % 8), (0, 0)))
                    pc = [(t, "full") for t in tabs_s8[2]] + [(g_qc[i], "full"), (kmean, "batch")]
                    o_c = _attn_sample("c", z3, 0, 2048, kc.reshape(Bs, Ts, LANES), 0, LANES, z3, ODD_VC // LANES,
                                       LANES, cache_c_k, cache_c_v, i, page_table, pc, SAMPLE_PAGES_PER_STEP)
                    o_d = _attn_sample("d", z3, 1, 2048, z3, ODD_KD // LANES, LANES, z3, ODD_VD // LANES, LANES,
                                       cache_d_k, cache_d_v, i, page_table, [(tri, "full")], SAMPLE_PAGES_PER_STEP)
                    o_c = o_c.reshape(Ms, -1).astype(BF16)
                    o_d = o_d.reshape(Ms, -1).astype(BF16)
                o, o2 = o_c, o_d
                w_out = w_out_o
                outs = (("c_k", kc), ("c_v", vc), ("d_k", kd), ("d_v", vd))
            for n, r in outs:
                new_rows[n][0 if name == "p" else 1].append(r)
            x = _matmul(o, w_out, i, out_dtype=F32, tn=512, res=x, gate=md, gate_spec=spec(2), x2=o2)
            h2 = _prenorm(x, md, spec(3), spec(4))
            u = _matmul(h2, w_up, l, out_dtype=BF16, tn=512, relu2=True)
            x = _matmul(u, w_down, l, out_dtype=F32, tk=2048, res=x, gate=md, gate_spec=spec(5))
            new_x[name] = x
        xp, xs = new_x["p"], new_x["s"]

    def stack(n):
        rp, rs = new_rows[n]
        return (jnp.stack(rp).reshape(len(rp), B, T, -1), jnp.stack(rs).reshape(len(rs), Bs, Ts, -1))

    out = [xp.reshape(B, T, D), xs.reshape(Bs, Ts, D)]
    for n in ("a_k", "a_v", "b_ckv", "b_kr", "c_k", "c_v", "d_k", "d_v"):
        out.extend(stack(n))
    return tuple(out)
```

```python
import functools
import math

import numpy as np
import jax
import jax.numpy as jnp
from jax import lax
from jax.experimental import pallas as pl
from jax.experimental.pallas import tpu as pltpu

F32 = jnp.float32
BF16 = jnp.bfloat16

EPS = 1e-6
ROPE_THETA = 500000.0
ADA_CHUNKS = 6
PAGE_SIZE = 128
MASKED_SCORE = -1e30
ALL_KEYS = 2 ** 30

LANES = 128
HEADS = 16
A_DIM = 64
A_ROT = 16
B_NOPE = 128
B_ROPE = 64
B_QK = B_NOPE + B_ROPE
B_HEAD_SLAB = 256
B_Q_LORA = 1024
B_KV_LORA = 256
B_Q_WIDTH = B_KV_LORA + LANES
C_DIM = 128
C_ROT = 32
D_DIM = 128
MOBA_BLOCK = 256
MOBA_TOPK = 3
MOBA_SHIFT = 8
KEY_BLOCK = 256
ROW_TILE = {"a": 128, "b": 256, "c": 128, "d": 128}
TILES_PER_TRIP = {"a": 64, "b": 8, "c": 16, "d": 16}
TILES_PER_STAGE = {"a": 1, "b": 8, "c": 1, "d": 16}
MAX_ROW_TILE = 256
SAMPLE_PAGES_PER_STEP = 16
SUFFIX_BLOCK = 128
LOG2_E = 1.4426950408889634
VMEM_LIMIT_BYTES = 56 * 1024 * 1024

EVEN_QA, EVEN_QLB, EVEN_KA, EVEN_VA, EVEN_CKV, EVEN_KR, EVEN_WIDTH = 0, 2048, 3072, 3200, 3328, 3584, 3840
ODD_QC, ODD_QD, ODD_KC, ODD_VC, ODD_KD, ODD_VD, ODD_WIDTH = 0, 2048, 4096, 4224, 4352, 4480, 4608


def _cparams(sem):
    return pltpu.CompilerParams(dimension_semantics=sem, vmem_limit_bytes=VMEM_LIMIT_BYTES)


def _dot(a, b):
    return jnp.dot(a, b, preferred_element_type=F32)


def _dot_nt(a, b):
    return lax.dot_general(a, b, (((1,), (1,)), ((), ())), preferred_element_type=F32)


def _split_bf16(x):
    hi = x.astype(BF16)
    lo = (x - hi.astype(F32)).astype(BF16)
    return hi, lo


def _mm_kernel(*refs, nk, silu, has_bias, relu2, has_res, has_x2):
    it = iter(refs)
    x_ref = next(it)
    x2_ref = next(it) if has_x2 else None
    w_ref = next(it)
    b_ref = next(it) if has_bias else None
    r_ref = next(it) if has_res else None
    g_ref = next(it) if has_res else None
    o_ref = next(it)
    acc_ref = next(it) if nk > 1 else None

    x = x_ref[...]
    if silu:
        x = (x * (1.0 / (1.0 + jnp.exp(-x)))).astype(BF16)
    if has_x2:
        k1 = x.shape[1]
        part = _dot(x, w_ref[:k1, :]) + _dot(x2_ref[...], w_ref[k1:, :])
    else:
        part = _dot(x, w_ref[...].astype(BF16))

    def epilogue(acc):
        if has_bias:
            acc = acc + b_ref[...]
        if relu2:
            acc = jnp.maximum(acc, 0.0)
            acc = acc * acc
        if has_res:
            acc = r_ref[...] + g_ref[...] * acc
        o_ref[...] = acc.astype(o_ref.dtype)

    if nk == 1:
        epilogue(part)
    else:
        k = pl.program_id(2)

        @pl.when(k == 0)
        def _():
            acc_ref[...] = part

        @pl.when(k > 0)
        def _():
            acc_ref[...] += part

        @pl.when(k == nk - 1)
        def _():
            epilogue(acc_ref[...])


def _pick_tile(n, pref, unit=LANES):
    if n <= pref:
        return n
    t = pref - pref % unit
    while n % t:
        t -= unit
    return t


def _matmul(x, w, layer, *, out_dtype, tm=1024, tn=1024, tk=4096, silu=False, bias=None, relu2=False,
            res=None, gate=None, gate_spec=None, x2=None):
    M, K = x.shape
    N = w.shape[-1]
    if x2 is not None:
        K = tk = K + x2.shape[1]
    tm, tn, tk = _pick_tile(M, tm), _pick_tile(N, tn), _pick_tile(K, tk)
    nk = K // tk
    in_specs = [pl.BlockSpec((tm, x.shape[1] if x2 is not None else tk), lambda i, j, k: (i, k))]
    args = [x]
    if x2 is not None:
        in_specs.append(pl.BlockSpec((tm, x2.shape[1]), lambda i, j, k: (i, 0)))
        args.append(x2)
    in_specs.append(pl.BlockSpec((None, tk, tn), lambda i, j, k: (layer, k, j)))
    args.append(w)
    if bias is not None:
        in_specs.append(pl.BlockSpec((1, tn), lambda i, j, k: (0, j)))
        args.append(bias)
    if res is not None:
        in_specs.append(pl.BlockSpec((tm, tn), lambda i, j, k: (i, j)))
        in_specs.append(gate_spec(tm, tn))
        args += [res, gate]
    scratch = [pltpu.VMEM((tm, tn), F32)] if nk > 1 else []
    kern = functools.partial(_mm_kernel, nk=nk, silu=silu, has_bias=bias is not None, relu2=relu2,
                             has_res=res is not None, has_x2=x2 is not None)
    return pl.pallas_call(
        kern, grid=(M // tm, N // tn, nk), in_specs=in_specs,
        out_specs=pl.BlockSpec((tm, tn), lambda i, j, k: (i, j)),
        out_shape=jax.ShapeDtypeStruct((M, N), out_dtype), scratch_shapes=scratch,
        compiler_params=_cparams(("parallel", "parallel", "arbitrary")), name="matmul")(*args)


def _prenorm_kernel(x_ref, shift_ref, scale_ref, o_ref):
    x = x_ref[...]
    y = x * lax.rsqrt(jnp.mean(x * x, axis=-1, keepdims=True) + EPS)
    o_ref[...] = (y * (1.0 + scale_ref[...]) + shift_ref[...]).astype(o_ref.dtype)


def _prenorm(x, mod, shift_spec, scale_spec, tm=256):
    M, D = x.shape
    tm = _pick_tile(M, tm)
    return pl.pallas_call(
        _prenorm_kernel, grid=(M // tm,),
        in_specs=[pl.BlockSpec((tm, D), lambda i: (i, 0)), shift_spec(tm), scale_spec(tm)],
        out_specs=pl.BlockSpec((tm, D), lambda i: (i, 0)),
        out_shape=jax.ShapeDtypeStruct((M, D), BF16),
        compiler_params=_cparams(("parallel",)), name="prenorm")(x, mod, mod)


def _lane_iota(shape):
    return lax.broadcasted_iota(jnp.int32, shape, len(shape) - 1)


def _rope(y, tab, half):
    c, s_up, s_dn = tab
    return y * c + pltpu.roll(y, LANES - half, 1) * s_up + pltpu.roll(y, half, 1) * s_dn


def _norm_pair64(x, gain):
    lo = _lane_iota(x.shape) < A_DIM
    sq = x * x
    s_lo = jnp.sum(jnp.where(lo, sq, 0.0), axis=-1, keepdims=True)
    s_hi = jnp.sum(jnp.where(lo, 0.0, sq), axis=-1, keepdims=True)
    inv = jnp.where(lo, lax.rsqrt(s_lo * (1.0 / A_DIM) + EPS), lax.rsqrt(s_hi * (1.0 / A_DIM) + EPS))
    return x * inv * gain


def _norm_rows(x, gain, n_real):
    ms = jnp.sum(x * x, axis=-1, keepdims=True) * (1.0 / n_real)
    return x * lax.rsqrt(ms + EPS) * gain


def _rope_tables(pos, rot, width, period):
    half = rot // 2
    inv = ROPE_THETA ** (-jnp.arange(half, dtype=F32) * 2.0 / rot)
    ang = pos.astype(F32)[:, None] * inv[None, :]
    cos, sin = jnp.cos(ang), jnp.sin(ang)
    lane = np.arange(LANES)
    g = lane % period
    first = (g < half) & (lane < width)
    second = (g >= half) & (g < rot) & (lane < width)
    idx = np.where(first, g, np.where(second, g - half, 0))
    cos_l, sin_l = cos[:, idx], sin[:, idx]
    rot_l = jnp.asarray(first | second)
    c = jnp.where(rot_l[None, :], cos_l, 1.0)
    s_up = jnp.where(jnp.asarray(first)[None, :], -sin_l, 0.0)
    s_dn = jnp.where(jnp.asarray(second)[None, :], sin_l, 0.0)
    return c, s_up, s_dn


def _prep_even_kernel(qlb_ref, ka_ref, ckv_ref, kr_ref, ca, sua, sda, cr, sur, sdr,
                      g_ka, g_qlb, g_ckv, g_kr, qlbn_o, ka_o, ckv_o, kr_o):
    qlb = qlb_ref[...]
    qlbn_o[...] = _norm_rows(qlb, g_qlb[...], B_Q_LORA).astype(qlbn_o.dtype)
    ka = _norm_pair64(ka_ref[...], g_ka[...])
    ka_o[...] = _rope(ka, (ca[...], sua[...], sda[...]), A_ROT // 2)
    ckv_o[...] = _norm_rows(ckv_ref[...], g_ckv[...], B_KV_LORA)
    kr = _norm_rows(kr_ref[...], g_kr[...], B_ROPE)
    kr = _rope(kr, (cr[...], sur[...], sdr[...]), B_ROPE // 2)
    kr_o[...] = kr


def _prep_odd_kernel(kc_ref, cc, suc, sdc, g_kc, kc_o):
    kc = _norm_rows(kc_ref[...], g_kc[...], C_DIM)
    kc_o[...] = _rope(kc, (cc[...], suc[...], sdc[...]), C_ROT // 2)


def _table_specs(tm, n_tab_blocks):
    return [pl.BlockSpec((tm, LANES), lambda i: (i % n_tab_blocks, 0))] * 3


def _full(shape):
    return pl.BlockSpec(shape, lambda *_: (0,) * len(shape))


def _prep_even(z, tab_a, tab_r, g_ka, g_qlb, g_ckv, g_kr, tm):
    M = z.shape[0]
    nt = tab_a[0].shape[0] // tm
    col = lambda width, start: pl.BlockSpec((tm, width), lambda i: (i, start // width))
    row = lambda width: pl.BlockSpec((tm, width), lambda i: (i, 0))
    return pl.pallas_call(
        _prep_even_kernel, grid=(M // tm,),
        in_specs=[col(B_Q_LORA, EVEN_QLB), col(LANES, EVEN_KA), col(B_KV_LORA, EVEN_CKV), col(LANES, EVEN_KR)]
        + _table_specs(tm, nt) + _table_specs(tm, nt)
        + [_full((1, LANES)), _full((1, B_Q_LORA)), _full((1, B_KV_LORA)), _full((1, LANES))],
        out_specs=[row(B_Q_LORA), row(LANES), row(B_KV_LORA), row(LANES)],
        out_shape=[jax.ShapeDtypeStruct((M, B_Q_LORA), BF16), jax.ShapeDtypeStruct((M, LANES), F32),
                   jax.ShapeDtypeStruct((M, B_KV_LORA), F32), jax.ShapeDtypeStruct((M, LANES), F32)],
        compiler_params=_cparams(("parallel",)), name="prep_even")(
            z, z, z, z, *tab_a, *tab_r, g_ka, g_qlb, g_ckv, g_kr)


def _prep_odd(z, tab_c, g_kc, tm):
    M = z.shape[0]
    nt = tab_c[0].shape[0] // tm
    return pl.pallas_call(
        _prep_odd_kernel, grid=(M // tm,),
        in_specs=[pl.BlockSpec((tm, LANES), lambda i: (i, ODD_KC // LANES))] + _table_specs(tm, nt)
        + [_full((1, LANES))],
        out_specs=pl.BlockSpec((tm, LANES), lambda i: (i, 0)),
        out_shape=jax.ShapeDtypeStruct((M, LANES), F32),
        compiler_params=_cparams(("parallel",)), name="prep_odd")(z, *tab_c, g_kc)


def _kmean_kernel(k_ref, o_ref):
    rows = k_ref[...]
    nb = rows.shape[0] // MOBA_BLOCK
    o_ref[...] = jnp.sum(rows.reshape(nb, MOBA_BLOCK, rows.shape[1]), axis=1) * (1.0 / MOBA_BLOCK)


def _page_sum_kernel(p_ref, o_ref):
    o_ref[...] = jnp.sum(p_ref[...], axis=1)


def _kmean_gather_kernel(pt_ref, sums_ref, o_ref, *, n_blocks):
    b = pl.program_id(0)
    per_block = MOBA_BLOCK // PAGE_SIZE
    for blk in range(n_blocks):
        acc = sums_ref[pl.ds(pt_ref[b, blk * per_block], 1), :]
        for j in range(1, per_block):
            acc = acc + sums_ref[pl.ds(pt_ref[b, blk * per_block + j], 1), :]
        o_ref[blk:blk + 1, :] = acc * (1.0 / MOBA_BLOCK)


def _kmean_prompt(kc, B, T):
    nb = T // MOBA_BLOCK
    out = pl.pallas_call(
        _kmean_kernel, grid=(B,),
        in_specs=[pl.BlockSpec((T, C_DIM), lambda b: (b, 0))],
        out_specs=pl.BlockSpec((None, nb, C_DIM), lambda b: (b, 0, 0)),
        out_shape=jax.ShapeDtypeStruct((B, nb, C_DIM), F32),
        compiler_params=_cparams(("parallel",)), name="kmean_prompt")(kc)
    return out


def _kmean_sample(pool, layer, page_table):
    Bs, n_pages = page_table.shape
    n_phys = pool.shape[1]
    group = _pick_tile(n_phys, 64, 8)
    sums = pl.pallas_call(
        _page_sum_kernel, grid=(n_phys // group,),
        in_specs=[pl.BlockSpec((None, group, PAGE_SIZE, C_DIM), lambda g: (layer, g, 0, 0))],
        out_specs=pl.BlockSpec((group, C_DIM), lambda g: (g, 0)),
        out_shape=jax.ShapeDtypeStruct((n_phys, C_DIM), F32),
        compiler_params=_cparams(("parallel",)), name="page_sums")(pool)
    n_blocks = n_pages * PAGE_SIZE // MOBA_BLOCK
    gs = pltpu.PrefetchScalarGridSpec(
        num_scalar_prefetch=1, grid=(Bs,),
        in_specs=[pl.BlockSpec((n_phys, C_DIM), lambda b, pt: (0, 0))],
        out_specs=pl.BlockSpec((None, n_blocks, C_DIM), lambda b, pt: (b, 0, 0)))
    return pl.pallas_call(
        functools.partial(_kmean_gather_kernel, n_blocks=n_blocks), grid_spec=gs,
        out_shape=jax.ShapeDtypeStruct((Bs, n_blocks, C_DIM), F32),
        compiler_params=_cparams(("parallel",)), name="kmean_sample")(page_table, sums)


def _moba_select(q_t, kmean, own):
    q_hi, q_lo = _split_bf16(q_t)
    k_hi, k_lo = _split_bf16(kmean)
    gate = _dot(k_hi, q_hi) + _dot(k_lo, q_hi) + _dot(k_hi, q_lo)
    slot = lax.broadcasted_iota(jnp.int32, gate.shape, 0)
    slot_f = slot.astype(F32)
    valid = slot < own
    gate = jnp.where(valid, gate, -jnp.inf)
    sel = jnp.zeros(gate.shape, F32)
    for _ in range(MOBA_TOPK):
        best = jnp.max(gate, axis=0, keepdims=True)
        first = jnp.min(jnp.where(gate == best, slot_f, float(gate.shape[0])), axis=0, keepdims=True)
        pick = slot_f == first
        sel = jnp.where(pick & valid, 1.0, sel)
        gate = jnp.where(pick, -jnp.inf, gate)
    return sel


def _suffix_sums(x, tri):
    n, w = x.shape
    pieces, later = [], None
    for b in reversed(range(n // SUFFIX_BLOCK)):
        xb = x[b * SUFFIX_BLOCK:(b + 1) * SUFFIX_BLOCK]
        hi, lo = _split_bf16(xb)
        both = _dot(tri, jnp.concatenate([hi, lo], axis=1))
        s = both[:, :w] + both[:, w:]
        total = s[0:1] + xb[0:1]
        if later is not None:
            s, total = s + later, total + later
        pieces.insert(0, s)
        later = total
    return jnp.concatenate(pieces, axis=0), later


def _attn_tiling(variant, M):
    rows = M * (2 if variant == "a" else 1)
    rt = min(ROW_TILE[variant], rows)
    return rt, rows // rt


def _attn_kernel(*refs, variant, tq, sample, pages, n_chunks, past_len, lam_init):
    M = HEADS * tq
    rt, n_tiles = _attn_tiling(variant, M)
    it = iter(refs)
    if sample:
        next(it)
    q_ref = next(it)
    k_ref = next(it)
    v_ref = next(it)
    if sample:
        kp_refs = [next(it) for _ in range(pages)]
        vp_refs = [next(it) for _ in range(pages)]
    tab = None
    if variant in "abc":
        tab = (next(it), next(it), next(it))
    if variant == "a":
        gq_ref, lam_ref, gsub_ref = next(it), next(it), next(it)
    elif variant == "b":
        gq_ref, wuk_ref, wuv_ref = next(it), next(it), next(it)
    elif variant == "c":
        gq_ref, kmean_ref = next(it), next(it)
    else:
        tri_ref = next(it)
    o_ref = next(it)
    q_scr = next(it)
    m1, l1, acc1 = next(it), next(it), next(it)
    if variant == "c":
        sel_scr = next(it)
    state_refs = (m1, acc1) if variant == "d" else (m1, l1, acc1)

    if sample:
        step = pl.program_id(1)
        q_base = past_len
        last_step = n_chunks - 1
    else:
        qi = pl.program_id(1)
        step = pl.program_id(2)
        q_base = qi * tq
        last_step = n_chunks - 1

    q_scale = {"a": A_DIM ** -0.5 * LOG2_E, "b": B_QK ** -0.5 * LOG2_E, "c": C_DIM ** -0.5 * LOG2_E,
               "d": D_DIM ** -0.5 * LOG2_E}[variant]

    @pl.when(step == 0)
    def _init():
        t = None if tab is None else tuple(r[...] for r in tab)
        pieces, second = [], []
        for h in range(HEADS):
            if variant == "a":
                y = _norm_pair64(q_ref[:, h * LANES:(h + 1) * LANES], gq_ref[...])
                y = _rope(y, t, A_ROT // 2) * q_scale
                lo = _lane_iota(y.shape) < A_DIM
                pieces.append(jnp.where(lo, y, 0.0))
                second.append(jnp.where(lo, 0.0, y))
            elif variant == "b":
                x = q_ref[:, h * B_HEAD_SLAB:(h + 1) * B_HEAD_SLAB]
                y = _norm_rows(x, gq_ref[...], B_QK)
                lat = _dot(y[:, :B_NOPE].astype(BF16), wuk_ref[h])
                rp = _rope(y[:, B_NOPE:], t, B_ROPE // 2)
                pieces.append(jnp.concatenate([lat, rp], axis=1) * q_scale)
            elif variant == "c":
                y = _norm_rows(q_ref[:, h * LANES:(h + 1) * LANES], gq_ref[...], C_DIM)
                y = _rope(y, t, C_ROT // 2)
                pieces.append(y)
            else:
                pieces.append(q_ref[:, h * LANES:(h + 1) * LANES] * q_scale)
        stacked = jnp.concatenate(pieces + second, axis=0)
        for r in range(n_tiles):
            x_t = stacked[r * rt:(r + 1) * rt].T
            if variant == "c":
                row = r * rt + lax.broadcasted_iota(jnp.int32, (1, rt), 1)
                own = lax.shift_right_arithmetic(q_base + (row & (tq - 1)), MOBA_SHIFT)
                sel_scr[r] = _moba_select(x_t, kmean_ref[...], own)
                x_t = x_t * q_scale
            q_scr[r] = x_t.astype(BF16)
        if variant == "d":
            m1[...] = jnp.zeros(m1.shape, F32)
        else:
            m1[...] = jnp.full(m1.shape, MASKED_SCORE, F32)
            l1[...] = jnp.zeros(l1.shape, F32)
        acc1[...] = jnp.zeros(acc1.shape, F32)

    def block(k, v, k_base, causal, v_is_t=False):
        n = k.shape[0]
        if variant == "b":
            kb = k.astype(BF16)
            v_t = k.T.astype(BF16)
            if v_is_t:
                v = jnp.concatenate([v, jnp.zeros((LANES - B_ROPE, n), F32)], axis=0).T
            kb = jnp.concatenate([kb, v.astype(BF16)], axis=1)
        else:
            kb, v_t = k.astype(BF16), v.T.astype(BF16)
        if variant == "c":
            slot = k_base // MOBA_BLOCK
        def positions(r):
            row = r * rt + lax.broadcasted_iota(jnp.int32, (1, rt), 1)
            return q_base + (row & (tq - 1)), k_base + lax.broadcasted_iota(jnp.int32, (n, rt), 0)

        def masked(r, s_t):
            qpos, kpos = positions(r)
            if variant == "c":
                own = lax.shift_right_arithmetic(qpos, MOBA_SHIFT)
                sel = sel_scr[r]
                slot_iota = lax.broadcasted_iota(jnp.int32, sel.shape, 0)
                span = min(n, MOBA_BLOCK)
                allowed = []
                for b in range(n // span):
                    hit = jnp.sum(jnp.where(slot_iota == slot + b, sel, 0.0), axis=0, keepdims=True)
                    limit = jnp.where(own == slot + b, qpos, jnp.where(hit > 0.5, ALL_KEYS, -1))
                    allowed.append(kpos[b * span:(b + 1) * span] <= limit)
                allowed = allowed[0] if len(allowed) == 1 else jnp.concatenate(allowed, axis=0)
                return jnp.where(allowed, s_t, MASKED_SCORE)
            return jnp.where(kpos <= qpos, s_t, MASKED_SCORE) if causal else s_t

        def stick_logs(r, z):
            log_beta = jnp.minimum(z, 0.0) - jnp.log(1.0 + jnp.exp2(-jnp.abs(z))) * LOG2_E
            log_1m = log_beta - z
            if not causal:
                return log_beta, log_1m, None
            qpos, kpos = positions(r)
            strict = kpos < qpos
            return log_beta, jnp.where(strict, log_1m, 0.0), strict

        group = min(n_tiles, TILES_PER_TRIP[variant])

        def staged(rs, states):
            scores = [_dot(kb, q_scr[r]) for r in rs]
            if variant == "d":
                logs = [stick_logs(r, z) for r, z in zip(rs, scores)]
                sums = [_suffix_sums(log_1m, tri_ref[...]) for _, log_1m, _ in logs]
                weights = []
                for (carry, _), (log_beta, _, strict), (newer, _) in zip(states, logs, sums):
                    a = jnp.exp2(log_beta + (carry + newer))
                    weights.append((jnp.where(strict, a, 0.0) if causal else a).astype(BF16))
                outs = [_dot(v_t, w) for w in weights]
                news = [(carry + total, acc + o) for (carry, acc), (_, total), o in zip(states, sums, outs)]
            else:
                parts = []
                for r, s_t, (m_old, l_old, _) in zip(rs, scores, states):
                    s_t = masked(r, s_t)
                    m_new = jnp.maximum(m_old, jnp.max(s_t, axis=0, keepdims=True))
                    alpha = jnp.exp2(m_old - m_new)
                    p = jnp.exp2(s_t - m_new)
                    parts.append((m_new, alpha * l_old + jnp.sum(p, axis=0, keepdims=True), alpha, p.astype(BF16)))
                outs = [_dot(v_t, p) for _, _, _, p in parts]
                news = [(m_new, l_new, alpha * acc + o)
                        for (m_new, l_new, alpha, _), (_, _, acc), o in zip(parts, states, outs)]
            return news

        def trip(t):
            rs = [t * group + g for g in range(group)]
            states = [tuple(ref[r] for ref in state_refs) for r in rs]
            chunk = min(group, TILES_PER_STAGE[variant])
            news = []
            for c in range(0, group, chunk):
                news += staged(rs[c:c + chunk], states[c:c + chunk])
            for r, new in zip(rs, news):
                for ref, val in zip(state_refs, new):
                    ref[r] = val

        if n_tiles == group:
            trip(0)
        else:
            def body(t, carry):
                trip(t)
                return carry
            lax.fori_loop(0, n_tiles // group, body, 0)

    if sample:
        @pl.when(step == 0)
        def _new_rows():
            pad = PAGE_SIZE - tq
            k = jnp.concatenate([k_ref[...], jnp.zeros((pad, k_ref.shape[1]), F32)], axis=0)
            v = jnp.concatenate([v_ref[...], jnp.zeros((pad, v_ref.shape[1]), F32)], axis=0)
            block(k, v, past_len, True)

        ch = n_chunks - 1 - step
        v_is_t = variant == "b"
        k = jnp.concatenate([r[...] for r in kp_refs], axis=0)
        v = jnp.concatenate([r[...] for r in vp_refs], axis=1 if v_is_t else 0)
        block(k, v, ch * (pages * PAGE_SIZE), False, v_is_t)
    else:
        @pl.when(step == 0)
        def _diag():
            block(k_ref[...], v_ref[...], qi * KEY_BLOCK, True)

        @pl.when((step > 0) & (step <= qi))
        def _past():
            block(k_ref[...], v_ref[...], (qi - step) * KEY_BLOCK, False)

    @pl.when(step == last_step)
    def _fin():
        if variant == "a":
            lp = lam_ref[...]
            lam = (jnp.exp(jnp.sum(lp[0:1] * lp[1:2], axis=-1, keepdims=True))
                   - jnp.exp(jnp.sum(lp[2:3] * lp[3:4], axis=-1, keepdims=True)) + lam_init)
        n_out = M // min(rt, M)
        w = min(rt, M)
        for r in range(n_out):
            if variant == "a":
                if n_tiles == 1:
                    both = acc1[0] / l1[0]
                    o_t = both[:, :M] - lam * both[:, M:]
                else:
                    r2 = r + n_out
                    o_t = acc1[r] / l1[r] - lam * (acc1[r2] / l1[r2])
                o_t = (o_t * lax.rsqrt(jnp.mean(o_t * o_t, axis=0, keepdims=True) + EPS)
                       * gsub_ref[:, :w] * (1.0 - lam_init))
            elif variant == "d":
                o_t = acc1[r]
            else:
                o_t = acc1[r] / l1[r]
            o = o_t.T
            for u in range(w // min(tq, w)):
                rows = min(tq, w)
                h, j = divmod(r * w + u * rows, tq)
                oh = o[u * rows:(u + 1) * rows]
                if variant == "b":
                    oh = _dot(oh.astype(BF16), wuv_ref[h])
                o_ref[j:j + rows, h * LANES:(h + 1) * LANES] = oh.astype(o_ref.dtype)


def _attn_scratch(variant, M, params):
    rt, n_tiles = _attn_tiling(variant, M)
    dq = B_Q_WIDTH if variant == "b" else LANES
    dv = B_KV_LORA if variant == "b" else LANES
    scr = [pltpu.VMEM((n_tiles, dq, rt), BF16), pltpu.VMEM((n_tiles, 1, rt), F32), pltpu.VMEM((n_tiles, 1, rt), F32),
           pltpu.VMEM((n_tiles, dv, rt), F32)]
    if variant == "c":
        n_slots = next(a.shape[1] for a, kind in params if kind == "batch")
        scr.append(pltpu.VMEM((n_tiles, n_slots, rt), F32))
    return scr


def _attn_prompt(variant, q_arr, q_col, q_width, k_arr, k_col, k_width, v_arr, v_col, v_width, params, B, T,
                 lam_init=0.0):
    tq = min(KEY_BLOCK, T)
    nq = T // tq
    M = HEADS * tq

    def kv_spec(col, width):
        return pl.BlockSpec((tq, width), lambda b, qi, ki: (b * nq + jnp.maximum(qi - ki, 0), col))

    in_specs = [pl.BlockSpec((tq, q_width), lambda b, qi, ki: (b * nq + qi, q_col)),
                kv_spec(k_col, k_width), kv_spec(v_col, v_width)]
    args = [q_arr, k_arr, v_arr]
    for arr, kind in params:
        if kind == "tab":
            in_specs.append(pl.BlockSpec((tq, LANES), lambda b, qi, ki: (qi, 0)))
        elif kind == "batch":
            in_specs.append(pl.BlockSpec((None,) + arr.shape[1:], lambda b, qi, ki: (b,) + (0,) * (arr.ndim - 1)))
        else:
            in_specs.append(_full(arr.shape))
        args.append(arr)
    kern = functools.partial(_attn_kernel, variant=variant, tq=tq, sample=False, pages=0, n_chunks=nq,
                             past_len=0, lam_init=lam_init)
    return pl.pallas_call(
        kern, grid=(B, nq, nq), in_specs=in_specs,
        out_specs=pl.BlockSpec((tq, HEADS * LANES), lambda b, qi, ki: (b * nq + qi, 0)),
        out_shape=jax.ShapeDtypeStruct((B * T, HEADS * LANES), BF16),
        scratch_shapes=_attn_scratch(variant, M, params),
        compiler_params=_cparams(("parallel", "parallel", "arbitrary")), name="attn_prompt_" + variant)(*args)


def _attn_sample(variant, q_arr, q_col, q_width, k_arr, k_col, k_width, v_arr, v_col, v_width,
                 k_pool, v_pool, layer, page_table, params, pages, lam_init=0.0):
    Bs, Ts = q_arr.shape[:2]
    n_pages = page_table.shape[1]
    pages = min(pages, n_pages)
    n_chunks = n_pages // pages
    M = HEADS * Ts

    def new_spec(col, width):
        return pl.BlockSpec((None, Ts, width), lambda b, c, pt: (b, 0, col))

    def page_spec(pool, j):
        return pl.BlockSpec((None, None) + pool.shape[2:],
                            lambda b, c, pt: (layer, pt[b, (n_chunks - 1 - c) * pages + j], 0, 0))

    in_specs = [new_spec(q_col, q_width), new_spec(k_col, k_width), new_spec(v_col, v_width)]
    in_specs += [page_spec(k_pool, j) for j in range(pages)]
    in_specs += [page_spec(v_pool, j) for j in range(pages)]
    args = [q_arr, k_arr, v_arr] + [k_pool] * pages + [v_pool] * pages
    for arr, kind in params:
        if kind == "batch":
            in_specs.append(pl.BlockSpec((None,) + arr.shape[1:], lambda b, c, pt: (b,) + (0,) * (arr.ndim - 1)))
        else:
            in_specs.append(_full(arr.shape))
        args.append(arr)
    gs = pltpu.PrefetchScalarGridSpec(
        num_scalar_prefetch=1, grid=(Bs, n_chunks), in_specs=in_specs,
        out_specs=pl.BlockSpec((None, Ts, HEADS * LANES), lambda b, c, pt: (b, 0, 0)),
        scratch_shapes=_attn_scratch(variant, M, params))
    kern = functools.partial(_attn_kernel, variant=variant, tq=Ts, sample=True, pages=pages, n_chunks=n_chunks,
                             past_len=n_pages * PAGE_SIZE, lam_init=lam_init)
    return pl.pallas_call(
        kern, grid_spec=gs, out_shape=jax.ShapeDtypeStruct((Bs, Ts, HEADS * LANES), F32),
        compiler_params=_cparams(("parallel", "arbitrary")), name="attn_sample_" + variant)(page_table, *args)


def _pad_last(a, n):
    return jnp.pad(a, [(0, 0)] * (a.ndim - 1) + [(0, n - a.shape[-1])])


def kernel(x_prompt, x_sample, cache_a_k, cache_a_v, cache_b_ckv, cache_b_kr, cache_c_k, cache_c_v, cache_d_k, cache_d_v, page_table, c_prompt, c_sample, w_ada, b_ada, ada_table, w_in_even, a_q_norm, a_k_norm, a_lambda_q1, a_lambda_k1, a_lambda_q2, a_lambda_k2, a_sub_norm, b_q_lat_norm, b_w_q_up, b_q_norm, b_kv_norm, b_kr_norm, b_w_uk, b_w_uv, w_out_even, w_in_odd, c_q_norm, c_k_norm, w_out_odd, w_mlp_up, w_mlp_down):
    B, T, D = x_prompt.shape
    Bs, Ts, _ = x_sample.shape
    depth = ada_table.shape[0]
    n_pages = page_table.shape[1]
    past_len = n_pages * PAGE_SIZE
    Mp, Ms = B * T, Bs * Ts
    n_even = w_in_even.shape[0]
    n_odd = w_in_odd.shape[0]

    wie = w_in_even
    w_in_e = jnp.concatenate(
        [wie[..., 0:2048], wie[..., 2304:3328], wie[..., 2048:2304], wie[..., 3328:3648],
         jnp.zeros(wie.shape[:2] + (EVEN_WIDTH - 3648,), wie.dtype)], axis=-1).astype(BF16)
    wio = w_in_odd
    w_in_o = jnp.concatenate([wio[..., 0:2048], wio[..., 2304:4352], wio[..., 2048:2304], wio[..., 4352:4608]],
                             axis=-1).astype(BF16)
    w_qup = _pad_last(b_w_q_up, B_HEAD_SLAB).reshape(n_even, B_Q_LORA, HEADS * B_HEAD_SLAB).astype(BF16)
    w_uk_t = jnp.transpose(b_w_uk, (0, 2, 3, 1)).astype(BF16)
    w_uv = jnp.transpose(b_w_uv, (0, 2, 1, 3)).astype(BF16)
    w_out_e = w_out_even.astype(BF16)
    w_out_o = w_out_odd.astype(BF16)
    w_up = w_mlp_up
    w_down = w_mlp_down.astype(BF16)
    w_ada_b = w_ada[None]

    g_qa = jnp.tile(a_q_norm, (1, 2))[:, None, :]
    g_ka = jnp.tile(a_k_norm, (1, 2))[:, None, :]
    g_sub = jnp.broadcast_to(a_sub_norm[:, :, None], a_sub_norm.shape + (MAX_ROW_TILE,))
    g_qlb = b_q_lat_norm[:, None, :]
    g_bq = _pad_last(b_q_norm, B_HEAD_SLAB)[:, None, :]
    g_ckv = b_kv_norm[:, None, :]
    g_kr = _pad_last(b_kr_norm, LANES)[:, None, :]
    g_qc = c_q_norm[:, None, :]
    g_kc = c_k_norm[:, None, :]
    lam_rows = jnp.stack([a_lambda_q1, a_lambda_k1, a_lambda_q2, a_lambda_k2], axis=1)
    lam_rows = jnp.pad(lam_rows, ((0, 0), (0, 4), (0, LANES - A_DIM)))
    tri = jnp.asarray(np.triu(np.ones((SUFFIX_BLOCK, SUFFIX_BLOCK), np.float32), 1), BF16)
    kr_pool_t = jnp.swapaxes(cache_b_kr, 2, 3)

    pos_p = jnp.arange(T)
    pos_s = past_len + jnp.arange(Ts)
    tm_p = _pick_tile(Mp, 256)
    tm_s = _pick_tile(Ms, 256)

    def tables(pos, tile_rows):
        reps = max(1, tile_rows // pos.shape[0])
        mk = lambda rot, width, period: tuple(jnp.tile(t, (reps, 1)) for t in _rope_tables(pos, rot, width, period))
        return mk(A_ROT, LANES, A_DIM), mk(B_ROPE, B_ROPE, LANES), mk(C_ROT, LANES, LANES)

    tabs_p = tables(pos_p, tm_p)
    tabs_s = tables(pos_s, tm_s)
    tabs_s8 = tables(pos_s, Ts)

    c_all = jnp.concatenate([c_prompt, c_sample], axis=0)
    n_c = c_all.shape[0]
    c_all = jnp.pad(c_all, ((0, (-n_c) % 16), (0, 0)))
    base = _matmul(c_all, w_ada_b, 0, out_dtype=F32, tm=c_all.shape[0], tn=1024, silu=True,
                   bias=b_ada[None, :])[:n_c].reshape(n_c, ADA_CHUNKS, D)

    xp = x_prompt.reshape(Mp, D)
    xs = x_sample.reshape(Ms, D)
    new_rows = {n: ([], []) for n in ("a_k", "a_v", "b_ckv", "b_kr", "c_k", "c_v", "d_k", "d_v")}

    for l in range(depth):
        i = l // 2
        mod = base + ada_table[l]
        mod_p = mod[:B].reshape(B * ADA_CHUNKS, 1, D)
        mod_s = jnp.repeat(jnp.transpose(mod[B:], (1, 0, 2)), Ts, axis=1)

        def p_spec(chunk, width=None):
            def make(tm, tn=D):
                per_batch = T // tm
                return pl.BlockSpec((None, 1, tn), lambda r, *jk: ((r // per_batch) * ADA_CHUNKS + chunk, 0,
                                                                    jk[0] if jk else 0))
            return make

        def s_spec(chunk):
            def make(tm, tn=D):
                return pl.BlockSpec((None, tm, tn), lambda r, *jk: (chunk, r, jk[0] if jk else 0))
            return make

        groups = (("p", xp, mod_p, p_spec), ("s", xs, mod_s, s_spec))
        new_x = {}
        for name, x, md, spec in groups:
            h = _prenorm(x, md, spec(0), spec(1))
            if l % 2 == 0:
                z = _matmul(h, w_in_e, i, out_dtype=F32)
                tabs = tabs_p if name == "p" else tabs_s
                tm = tm_p if name == "p" else tm_s
                qlbn, ka, ckv, kr = _prep_even(z, tabs[0], tabs[1], g_ka[i], g_qlb[i], g_ckv[i], g_kr[i], tm)
                qh = _matmul(qlbn, w_qup, i, out_dtype=F32)
                lam_init = 0.8 - 0.6 * math.exp(-0.3 * l)
                if name == "p":
                    pa = [(t, "tab") for t in tabs_p[0]] + [(g_qa[i], "full"), (lam_rows[i], "full"), (g_sub[i], "full")]
                    o_a = _attn_prompt("a", z, 0, 2048, ka, 0, LANES, z, EVEN_VA // LANES, LANES, pa, B, T, lam_init)
                    pb = [(t, "tab") for t in tabs_p[1]] + [(g_bq[i], "full"), (w_uk_t[i], "full"), (w_uv[i], "full")]
                    o_b = _attn_prompt("b", qh, 0, HEADS * B_HEAD_SLAB, ckv, 0, B_KV_LORA, kr, 0, LANES, pb, B, T)
                    va = z[:, EVEN_VA:EVEN_VA + LANES]
                else:
                    z3 = z.reshape(Bs, Ts, EVEN_WIDTH)
                    pa = [(t, "full") for t in tabs_s8[0]] + [(g_qa[i], "full"), (lam_rows[i], "full"), (g_sub[i], "full")]
                    o_a = _attn_sample("a", z3, 0, 2048, ka.reshape(Bs, Ts, LANES), 0, LANES, z3, EVEN_VA // LANES,
                                       LANES, cache_a_k, cache_a_v, i, page_table, pa, SAMPLE_PAGES_PER_STEP, lam_init)
                    pb = [(t, "full") for t in tabs_s8[1]] + [(g_bq[i], "full"), (w_uk_t[i], "full"), (w_uv[i], "full")]
                    o_b = _attn_sample("b", qh.reshape(Bs, Ts, -1), 0, HEADS * B_HEAD_SLAB,
                                       ckv.reshape(Bs, Ts, B_KV_LORA), 0, B_KV_LORA, kr.reshape(Bs, Ts, LANES), 0,
                                       LANES, cache_b_ckv, kr_pool_t, i, page_table, pb, SAMPLE_PAGES_PER_STEP)
                    o_a = o_a.reshape(Ms, -1).astype(BF16)
                    o_b = o_b.reshape(Ms, -1).astype(BF16)
                    va = z[:, EVEN_VA:EVEN_VA + LANES]
                o, o2 = o_a, o_b
                w_out = w_out_e
                outs = (("a_k", ka), ("a_v", va), ("b_ckv", ckv), ("b_kr", kr[:, :B_ROPE]))
            else:
                z = _matmul(h, w_in_o, i, out_dtype=F32)
                tabs = tabs_p if name == "p" else tabs_s
                tm = tm_p if name == "p" else tm_s
                kc = _prep_odd(z, tabs[2], g_kc[i], tm)
                vc = z[:, ODD_VC:ODD_VC + LANES]
                kd = z[:, ODD_KD:ODD_KD + LANES]
                vd = z[:, ODD_VD:ODD_VD + LANES]
                if name == "p":
                    kmean = _kmean_prompt(kc, B, T)
                    kmean = jnp.pad(kmean, ((0, 0), (0, (-kmean.shape[1]) % 8), (0, 0)))
                    pc = [(t, "tab") for t in tabs_p[2]] + [(g_qc[i], "full"), (kmean, "batch")]
                    o_c = _attn_prompt("c", z, 0, 2048, kc, 0, LANES, z, ODD_VC // LANES, LANES, pc, B, T)
                    o_d = _attn_prompt("d", z, 1, 2048, z, ODD_KD // LANES, LANES, z, ODD_VD // LANES, LANES,
                                       [(tri, "full")], B, T)
                else:
                    z3 = z.reshape(Bs, Ts, ODD_WIDTH)
                    kmean = _kmean_sample(cache_c_k, i, page_table)
                    kmean = jnp.pad(kmean, ((0, 0), (0, (-kmean.shape[1]) % 8), (0, 0)))
                    pc = [(t, "full") for t in tabs_s8[2]] + [(g_qc[i], "full"), (kmean, "batch")]
                    o_c = _attn_sample("c", z3, 0, 2048, kc.reshape(Bs, Ts, LANES), 0, LANES, z3, ODD_VC // LANES,
                                       LANES, cache_c_k, cache_c_v, i, page_table, pc, SAMPLE_PAGES_PER_STEP)
                    o_d = _attn_sample("d", z3, 1, 2048, z3, ODD_KD // LANES, LANES, z3, ODD_VD // LANES, LANES,
                                       cache_d_k, cache_d_v, i, page_table, [(tri, "full")], SAMPLE_PAGES_PER_STEP)
                    o_c = o_c.reshape(Ms, -1).astype(BF16)
                    o_d = o_d.reshape(Ms, -1).astype(BF16)
                o, o2 = o_c, o_d
                w_out = w_out_o
                outs = (("c_k", kc), ("c_v", vc), ("d_k", kd), ("d_v", vd))
            for n, r in outs:
                new_rows[n][0 if name == "p" else 1].append(r)
            x = _matmul(o, w_out, i, out_dtype=F32, tn=512, res=x, gate=md, gate_spec=spec(2), x2=o2)
            h2 = _prenorm(x, md, spec(3), spec(4))
            u = _matmul(h2, w_up, l, out_dtype=BF16, tn=512, relu2=True)
            x = _matmul(u, w_down, l, out_dtype=F32, tk=2048, res=x, gate=md, gate_spec=spec(5))
            new_x[name] = x
        xp, xs = new_x["p"], new_x["s"]

    def stack(n):
        rp, rs = new_rows[n]
        return (jnp.stack(rp).reshape(len(rp), B, T, -1), jnp.stack(rs).reshape(len(rs), Bs, Ts, -1))

    out = [xp.reshape(B, T, D), xs.reshape(Bs, Ts, D)]
    for n in ("a_k", "a_v", "b_ckv", "b_kr", "c_k", "c_v", "d_k", "d_v"):
        out.extend(stack(n))
    return tuple(out)
```

```python
import functools
import math

import numpy as np
import jax
import jax.numpy as jnp
from jax import lax
from jax.experimental import pallas as pl
from jax.experimental.pallas import tpu as pltpu

F32 = jnp.float32
BF16 = jnp.bfloat16

EPS = 1e-6
ROPE_THETA = 500000.0
ADA_CHUNKS = 6
PAGE_SIZE = 128
MASKED_SCORE = -1e30
ALL_KEYS = 2 ** 30

LANES = 128
HEADS = 16
A_DIM = 64
A_ROT = 16
B_NOPE = 128
B_ROPE = 64
B_QK = B_NOPE + B_ROPE
B_HEAD_SLAB = 256
B_Q_LORA = 1024
B_KV_LORA = 256
B_Q_WIDTH = B_KV_LORA + LANES
C_DIM = 128
C_ROT = 32
D_DIM = 128
MOBA_BLOCK = 256
MOBA_TOPK = 3
MOBA_SHIFT = 8
KEY_BLOCK = 256
ROW_TILE = {"a": 128, "b": 256, "c": 128, "d": 128}
TILES_PER_TRIP = {"a": 64, "b": 8, "c": 16, "d": 16}
TILES_PER_STAGE = {"a": 1, "b": 8, "c": 1, "d": 16}
MAX_ROW_TILE = 256
SAMPLE_PAGES_PER_STEP = 32
SUFFIX_BLOCK = 128
LOG2_E = 1.4426950408889634
VMEM_LIMIT_BYTES = 56 * 1024 * 1024

EVEN_QA, EVEN_QLB, EVEN_KA, EVEN_VA, EVEN_CKV, EVEN_KR, EVEN_WIDTH = 0, 2048, 3072, 3200, 3328, 3584, 3840
ODD_QC, ODD_QD, ODD_KC, ODD_VC, ODD_KD, ODD_VD, ODD_WIDTH = 0, 2048, 4096, 4224, 4352, 4480, 4608


def _cparams(sem):
    return pltpu.CompilerParams(dimension_semantics=sem, vmem_limit_bytes=VMEM_LIMIT_BYTES)


def _dot(a, b):
    return jnp.dot(a, b, preferred_element_type=F32)


def _dot_nt(a, b):
    return lax.dot_general(a, b, (((1,), (1,)), ((), ())), preferred_element_type=F32)


def _split_bf16(x):
    hi = x.astype(BF16)
    lo = (x - hi.astype(F32)).astype(BF16)
    return hi, lo


def _mm_kernel(*refs, nk, silu, has_bias, relu2, has_res, has_x2):
    it = iter(refs)
    x_ref = next(it)
    x2_ref = next(it) if has_x2 else None
    w_ref = next(it)
    b_ref = next(it) if has_bias else None
    r_ref = next(it) if has_res else None
    g_ref = next(it) if has_res else None
    o_ref = next(it)
    acc_ref = next(it) if nk > 1 else None

    x = x_ref[...]
    if silu:
        x = (x * (1.0 / (1.0 + jnp.exp(-x)))).astype(BF16)
    if has_x2:
        k1 = x.shape[1]
        part = _dot(x, w_ref[:k1, :]) + _dot(x2_ref[...], w_ref[k1:, :])
    else:
        part = _dot(x, w_ref[...].astype(BF16))

    def epilogue(acc):
        if has_bias:
            acc = acc + b_ref[...]
        if relu2:
            acc = jnp.maximum(acc, 0.0)
            acc = acc * acc
        if has_res:
            acc = r_ref[...] + g_ref[...] * acc
        o_ref[...] = acc.astype(o_ref.dtype)

    if nk == 1:
        epilogue(part)
    else:
        k = pl.program_id(2)

        @pl.when(k == 0)
        def _():
            acc_ref[...] = part

        @pl.when(k > 0)
        def _():
            acc_ref[...] += part

        @pl.when(k == nk - 1)
        def _():
            epilogue(acc_ref[...])


def _pick_tile(n, pref, unit=LANES):
    if n <= pref:
        return n
    t = pref - pref % unit
    while n % t:
        t -= unit
    return t


def _matmul(x, w, layer, *, out_dtype, tm=1024, tn=1024, tk=4096, silu=False, bias=None, relu2=False,
            res=None, gate=None, gate_spec=None, x2=None):
    M, K = x.shape
    N = w.shape[-1]
    if x2 is not None:
        K = tk = K + x2.shape[1]
    tm, tn, tk = _pick_tile(M, tm), _pick_tile(N, tn), _pick_tile(K, tk)
    nk = K // tk
    in_specs = [pl.BlockSpec((tm, x.shape[1] if x2 is not None else tk), lambda i, j, k: (i, k))]
    args = [x]
    if x2 is not None:
        in_specs.append(pl.BlockSpec((tm, x2.shape[1]), lambda i, j, k: (i, 0)))
        args.append(x2)
    in_specs.append(pl.BlockSpec((None, tk, tn), lambda i, j, k: (layer, k, j)))
    args.append(w)
    if bias is not None:
        in_specs.append(pl.BlockSpec((1, tn), lambda i, j, k: (0, j)))
        args.append(bias)
    if res is not None:
        in_specs.append(pl.BlockSpec((tm, tn), lambda i, j, k: (i, j)))
        in_specs.append(gate_spec(tm, tn))
        args += [res, gate]
    scratch = [pltpu.VMEM((tm, tn), F32)] if nk > 1 else []
    kern = functools.partial(_mm_kernel, nk=nk, silu=silu, has_bias=bias is not None, relu2=relu2,
                             has_res=res is not None, has_x2=x2 is not None)
    return pl.pallas_call(
        kern, grid=(M // tm, N // tn, nk), in_specs=in_specs,
        out_specs=pl.BlockSpec((tm, tn), lambda i, j, k: (i, j)),
        out_shape=jax.ShapeDtypeStruct((M, N), out_dtype), scratch_shapes=scratch,
        compiler_params=_cparams(("parallel", "parallel", "arbitrary")), name="matmul")(*args)


def _prenorm_kernel(x_ref, shift_ref, scale_ref, o_ref):
    x = x_ref[...]
    y = x * lax.rsqrt(jnp.mean(x * x, axis=-1, keepdims=True) + EPS)
    o_ref[...] = (y * (1.0 + scale_ref[...]) + shift_ref[...]).astype(o_ref.dtype)


def _prenorm(x, mod, shift_spec, scale_spec, tm=256):
    M, D = x.shape
    tm = _pick_tile(M, tm)
    return pl.pallas_call(
        _prenorm_kernel, grid=(M // tm,),
        in_specs=[pl.BlockSpec((tm, D), lambda i: (i, 0)), shift_spec(tm), scale_spec(tm)],
        out_specs=pl.BlockSpec((tm, D), lambda i: (i, 0)),
        out_shape=jax.ShapeDtypeStruct((M, D), BF16),
        compiler_params=_cparams(("parallel",)), name="prenorm")(x, mod, mod)


def _lane_iota(shape):
    return lax.broadcasted_iota(jnp.int32, shape, len(shape) - 1)


def _rope(y, tab, half):
    c, s_up, s_dn = tab
    return y * c + pltpu.roll(y, LANES - half, 1) * s_up + pltpu.roll(y, half, 1) * s_dn


def _norm_pair64(x, gain):
    lo = _lane_iota(x.shape) < A_DIM
    sq = x * x
    s_lo = jnp.sum(jnp.where(lo, sq, 0.0), axis=-1, keepdims=True)
    s_hi = jnp.sum(jnp.where(lo, 0.0, sq), axis=-1, keepdims=True)
    inv = jnp.where(lo, lax.rsqrt(s_lo * (1.0 / A_DIM) + EPS), lax.rsqrt(s_hi * (1.0 / A_DIM) + EPS))
    return x * inv * gain


def _norm_rows(x, gain, n_real):
    ms = jnp.sum(x * x, axis=-1, keepdims=True) * (1.0 / n_real)
    return x * lax.rsqrt(ms + EPS) * gain


def _rope_tables(pos, rot, width, period):
    half = rot // 2
    inv = ROPE_THETA ** (-jnp.arange(half, dtype=F32) * 2.0 / rot)
    ang = pos.astype(F32)[:, None] * inv[None, :]
    cos, sin = jnp.cos(ang), jnp.sin(ang)
    lane = np.arange(LANES)
    g = lane % period
    first = (g < half) & (lane < width)
    second = (g >= half) & (g < rot) & (lane < width)
    idx = np.where(first, g, np.where(second, g - half, 0))
    cos_l, sin_l = cos[:, idx], sin[:, idx]
    rot_l = jnp.asarray(first | second)
    c = jnp.where(rot_l[None, :], cos_l, 1.0)
    s_up = jnp.where(jnp.asarray(first)[None, :], -sin_l, 0.0)
    s_dn = jnp.where(jnp.asarray(second)[None, :], sin_l, 0.0)
    return c, s_up, s_dn


def _prep_even_kernel(qlb_ref, ka_ref, ckv_ref, kr_ref, ca, sua, sda, cr, sur, sdr,
                      g_ka, g_qlb, g_ckv, g_kr, qlbn_o, ka_o, ckv_o, kr_o):
    qlb = qlb_ref[...]
    qlbn_o[...] = _norm_rows(qlb, g_qlb[...], B_Q_LORA).astype(qlbn_o.dtype)
    ka = _norm_pair64(ka_ref[...], g_ka[...])
    ka_o[...] = _rope(ka, (ca[...], sua[...], sda[...]), A_ROT // 2)
    ckv_o[...] = _norm_rows(ckv_ref[...], g_ckv[...], B_KV_LORA)
    kr = _norm_rows(kr_ref[...], g_kr[...], B_ROPE)
    kr = _rope(kr, (cr[...], sur[...], sdr[...]), B_ROPE // 2)
    kr_o[...] = kr


def _prep_odd_kernel(kc_ref, cc, suc, sdc, g_kc, kc_o):
    kc = _norm_rows(kc_ref[...], g_kc[...], C_DIM)
    kc_o[...] = _rope(kc, (cc[...], suc[...], sdc[...]), C_ROT // 2)


def _table_specs(tm, n_tab_blocks):
    return [pl.BlockSpec((tm, LANES), lambda i: (i % n_tab_blocks, 0))] * 3


def _full(shape):
    return pl.BlockSpec(shape, lambda *_: (0,) * len(shape))


def _prep_even(z, tab_a, tab_r, g_ka, g_qlb, g_ckv, g_kr, tm):
    M = z.shape[0]
    nt = tab_a[0].shape[0] // tm
    col = lambda width, start: pl.BlockSpec((tm, width), lambda i: (i, start // width))
    row = lambda width: pl.BlockSpec((tm, width), lambda i: (i, 0))
    return pl.pallas_call(
        _prep_even_kernel, grid=(M // tm,),
        in_specs=[col(B_Q_LORA, EVEN_QLB), col(LANES, EVEN_KA), col(B_KV_LORA, EVEN_CKV), col(LANES, EVEN_KR)]
        + _table_specs(tm, nt) + _table_specs(tm, nt)
        + [_full((1, LANES)), _full((1, B_Q_LORA)), _full((1, B_KV_LORA)), _full((1, LANES))],
        out_specs=[row(B_Q_LORA), row(LANES), row(B_KV_LORA), row(LANES)],
        out_shape=[jax.ShapeDtypeStruct((M, B_Q_LORA), BF16), jax.ShapeDtypeStruct((M, LANES), F32),
                   jax.ShapeDtypeStruct((M, B_KV_LORA), F32), jax.ShapeDtypeStruct((M, LANES), F32)],
        compiler_params=_cparams(("parallel",)), name="prep_even")(
            z, z, z, z, *tab_a, *tab_r, g_ka, g_qlb, g_ckv, g_kr)


def _prep_odd(z, tab_c, g_kc, tm):
    M = z.shape[0]
    nt = tab_c[0].shape[0] // tm
    return pl.pallas_call(
        _prep_odd_kernel, grid=(M // tm,),
        in_specs=[pl.BlockSpec((tm, LANES), lambda i: (i, ODD_KC // LANES))] + _table_specs(tm, nt)
        + [_full((1, LANES))],
        out_specs=pl.BlockSpec((tm, LANES), lambda i: (i, 0)),
        out_shape=jax.ShapeDtypeStruct((M, LANES), F32),
        compiler_params=_cparams(("parallel",)), name="prep_odd")(z, *tab_c, g_kc)


def _kmean_kernel(k_ref, o_ref):
    rows = k_ref[...]
    nb = rows.shape[0] // MOBA_BLOCK
    o_ref[...] = jnp.sum(rows.reshape(nb, MOBA_BLOCK, rows.shape[1]), axis=1) * (1.0 / MOBA_BLOCK)


def _page_sum_kernel(p_ref, o_ref):
    o_ref[...] = jnp.sum(p_ref[...], axis=1)


def _kmean_gather_kernel(pt_ref, sums_ref, o_ref, *, n_blocks):
    b = pl.program_id(0)
    per_block = MOBA_BLOCK // PAGE_SIZE
    for blk in range(n_blocks):
        acc = sums_ref[pl.ds(pt_ref[b, blk * per_block], 1), :]
        for j in range(1, per_block):
            acc = acc + sums_ref[pl.ds(pt_ref[b, blk * per_block + j], 1), :]
        o_ref[blk:blk + 1, :] = acc * (1.0 / MOBA_BLOCK)


def _kmean_prompt(kc, B, T):
    nb = T // MOBA_BLOCK
    out = pl.pallas_call(
        _kmean_kernel, grid=(B,),
        in_specs=[pl.BlockSpec((T, C_DIM), lambda b: (b, 0))],
        out_specs=pl.BlockSpec((None, nb, C_DIM), lambda b: (b, 0, 0)),
        out_shape=jax.ShapeDtypeStruct((B, nb, C_DIM), F32),
        compiler_params=_cparams(("parallel",)), name="kmean_prompt")(kc)
    return out


def _kmean_sample(pool, layer, page_table):
    Bs, n_pages = page_table.shape
    n_phys = pool.shape[1]
    group = _pick_tile(n_phys, 64, 8)
    sums = pl.pallas_call(
        _page_sum_kernel, grid=(n_phys // group,),
        in_specs=[pl.BlockSpec((None, group, PAGE_SIZE, C_DIM), lambda g: (layer, g, 0, 0))],
        out_specs=pl.BlockSpec((group, C_DIM), lambda g: (g, 0)),
        out_shape=jax.ShapeDtypeStruct((n_phys, C_DIM), F32),
        compiler_params=_cparams(("parallel",)), name="page_sums")(pool)
    n_blocks = n_pages * PAGE_SIZE // MOBA_BLOCK
    gs = pltpu.PrefetchScalarGridSpec(
        num_scalar_prefetch=1, grid=(Bs,),
        in_specs=[pl.BlockSpec((n_phys, C_DIM), lambda b, pt: (0, 0))],
        out_specs=pl.BlockSpec((None, n_blocks, C_DIM), lambda b, pt: (b, 0, 0)))
    return pl.pallas_call(
        functools.partial(_kmean_gather_kernel, n_blocks=n_blocks), grid_spec=gs,
        out_shape=jax.ShapeDtypeStruct((Bs, n_blocks, C_DIM), F32),
        compiler_params=_cparams(("parallel",)), name="kmean_sample")(page_table, sums)


def _moba_select(q_t, kmean, own):
    q_hi, q_lo = _split_bf16(q_t)
    k_hi, k_lo = _split_bf16(kmean)
    gate = _dot(k_hi, q_hi) + _dot(k_lo, q_hi) + _dot(k_hi, q_lo)
    slot = lax.broadcasted_iota(jnp.int32, gate.shape, 0)
    slot_f = slot.astype(F32)
    valid = slot < own
    gate = jnp.where(valid, gate, -jnp.inf)
    sel = jnp.zeros(gate.shape, F32)
    for _ in range(MOBA_TOPK):
        best = jnp.max(gate, axis=0, keepdims=True)
        first = jnp.min(jnp.where(gate == best, slot_f, float(gate.shape[0])), axis=0, keepdims=True)
        pick = slot_f == first
        sel = jnp.where(pick & valid, 1.0, sel)
        gate = jnp.where(pick, -jnp.inf, gate)
    return sel


def _suffix_sums(x, tri):
    n, w = x.shape
    pieces, later = [], None
    for b in reversed(range(n // SUFFIX_BLOCK)):
        xb = x[b * SUFFIX_BLOCK:(b + 1) * SUFFIX_BLOCK]
        hi, lo = _split_bf16(xb)
        both = _dot(tri, jnp.concatenate([hi, lo], axis=1))
        s = both[:, :w] + both[:, w:]
        total = s[0:1] + xb[0:1]
        if later is not None:
            s, total = s + later, total + later
        pieces.insert(0, s)
        later = total
    return jnp.concatenate(pieces, axis=0), later


def _attn_tiling(variant, M):
    rows = M * (2 if variant == "a" else 1)
    rt = min(ROW_TILE[variant], rows)
    return rt, rows // rt


def _attn_kernel(*refs, variant, tq, sample, pages, n_chunks, past_len, lam_init):
    M = HEADS * tq
    rt, n_tiles = _attn_tiling(variant, M)
    it = iter(refs)
    if sample:
        next(it)
    q_ref = next(it)
    k_ref = next(it)
    v_ref = next(it)
    if sample:
        kp_refs = [next(it) for _ in range(pages)]
        vp_refs = [next(it) for _ in range(pages)]
    tab = None
    if variant in "abc":
        tab = (next(it), next(it), next(it))
    if variant == "a":
        gq_ref, lam_ref, gsub_ref = next(it), next(it), next(it)
    elif variant == "b":
        gq_ref, wuk_ref, wuv_ref = next(it), next(it), next(it)
    elif variant == "c":
        gq_ref, kmean_ref = next(it), next(it)
    else:
        tri_ref = next(it)
    o_ref = next(it)
    q_scr = next(it)
    m1, l1, acc1 = next(it), next(it), next(it)
    if variant == "c":
        sel_scr = next(it)
    state_refs = (m1, acc1) if variant == "d" else (m1, l1, acc1)

    if sample:
        step = pl.program_id(1)
        q_base = past_len
        last_step = n_chunks - 1
    else:
        qi = pl.program_id(1)
        step = pl.program_id(2)
        q_base = qi * tq
        last_step = n_chunks - 1

    q_scale = {"a": A_DIM ** -0.5 * LOG2_E, "b": B_QK ** -0.5 * LOG2_E, "c": C_DIM ** -0.5 * LOG2_E,
               "d": D_DIM ** -0.5 * LOG2_E}[variant]

    @pl.when(step == 0)
    def _init():
        t = None if tab is None else tuple(r[...] for r in tab)
        pieces, second = [], []
        for h in range(HEADS):
            if variant == "a":
                y = _norm_pair64(q_ref[:, h * LANES:(h + 1) * LANES], gq_ref[...])
                y = _rope(y, t, A_ROT // 2) * q_scale
                lo = _lane_iota(y.shape) < A_DIM
                pieces.append(jnp.where(lo, y, 0.0))
                second.append(jnp.where(lo, 0.0, y))
            elif variant == "b":
                x = q_ref[:, h * B_HEAD_SLAB:(h + 1) * B_HEAD_SLAB]
                y = _norm_rows(x, gq_ref[...], B_QK)
                lat = _dot(y[:, :B_NOPE].astype(BF16), wuk_ref[h])
                rp = _rope(y[:, B_NOPE:], t, B_ROPE // 2)
                pieces.append(jnp.concatenate([lat, rp], axis=1) * q_scale)
            elif variant == "c":
                y = _norm_rows(q_ref[:, h * LANES:(h + 1) * LANES], gq_ref[...], C_DIM)
                y = _rope(y, t, C_ROT // 2)
                pieces.append(y)
            else:
                pieces.append(q_ref[:, h * LANES:(h + 1) * LANES] * q_scale)
        stacked = jnp.concatenate(pieces + second, axis=0)
        for r in range(n_tiles):
            x_t = stacked[r * rt:(r + 1) * rt].T
            if variant == "c":
                row = r * rt + lax.broadcasted_iota(jnp.int32, (1, rt), 1)
                own = lax.shift_right_arithmetic(q_base + (row & (tq - 1)), MOBA_SHIFT)
                sel_scr[r] = _moba_select(x_t, kmean_ref[...], own)
                x_t = x_t * q_scale
            q_scr[r] = x_t.astype(BF16)
        if variant == "d":
            m1[...] = jnp.zeros(m1.shape, F32)
        else:
            m1[...] = jnp.full(m1.shape, MASKED_SCORE, F32)
            l1[...] = jnp.zeros(l1.shape, F32)
        acc1[...] = jnp.zeros(acc1.shape, F32)

    def block(k, v, k_base, causal, v_is_t=False):
        n = k.shape[0]
        if variant == "b":
            kb = k.astype(BF16)
            v_t = k.T.astype(BF16)
            if v_is_t:
                v = jnp.concatenate([v, jnp.zeros((LANES - B_ROPE, n), F32)], axis=0).T
            kb = jnp.concatenate([kb, v.astype(BF16)], axis=1)
        else:
            kb, v_t = k.astype(BF16), v.T.astype(BF16)
        if variant == "c":
            slot = k_base // MOBA_BLOCK
        def positions(r):
            row = r * rt + lax.broadcasted_iota(jnp.int32, (1, rt), 1)
            return q_base + (row & (tq - 1)), k_base + lax.broadcasted_iota(jnp.int32, (n, rt), 0)

        def masked(r, s_t):
            qpos, kpos = positions(r)
            if variant == "c":
                own = lax.shift_right_arithmetic(qpos, MOBA_SHIFT)
                sel = sel_scr[r]
                slot_iota = lax.broadcasted_iota(jnp.int32, sel.shape, 0)
                span = min(n, MOBA_BLOCK)
                allowed = []
                for b in range(n // span):
                    hit = jnp.sum(jnp.where(slot_iota == slot + b, sel, 0.0), axis=0, keepdims=True)
                    limit = jnp.where(own == slot + b, qpos, jnp.where(hit > 0.5, ALL_KEYS, -1))
                    allowed.append(kpos[b * span:(b + 1) * span] <= limit)
                allowed = allowed[0] if len(allowed) == 1 else jnp.concatenate(allowed, axis=0)
                return jnp.where(allowed, s_t, MASKED_SCORE)
            return jnp.where(kpos <= qpos, s_t, MASKED_SCORE) if causal else s_t

        def stick_logs(r, z):
            log_beta = jnp.minimum(z, 0.0) - jnp.log(1.0 + jnp.exp2(-jnp.abs(z))) * LOG2_E
            log_1m = log_beta - z
            if not causal:
                return log_beta, log_1m, None
            qpos, kpos = positions(r)
            strict = kpos < qpos
            return log_beta, jnp.where(strict, log_1m, 0.0), strict

        group = min(n_tiles, TILES_PER_TRIP[variant])

        def staged(rs, states):
            scores = [_dot(kb, q_scr[r]) for r in rs]
            if variant == "d":
                logs = [stick_logs(r, z) for r, z in zip(rs, scores)]
                sums = [_suffix_sums(log_1m, tri_ref[...]) for _, log_1m, _ in logs]
                weights = []
                for (carry, _), (log_beta, _, strict), (newer, _) in zip(states, logs, sums):
                    a = jnp.exp2(log_beta + (carry + newer))
                    weights.append((jnp.where(strict, a, 0.0) if causal else a).astype(BF16))
                outs = [_dot(v_t, w) for w in weights]
                news = [(carry + total, acc + o) for (carry, acc), (_, total), o in zip(states, sums, outs)]
            else:
                parts = []
                for r, s_t, (m_old, l_old, _) in zip(rs, scores, states):
                    s_t = masked(r, s_t)
                    m_new = jnp.maximum(m_old, jnp.max(s_t, axis=0, keepdims=True))
                    alpha = jnp.exp2(m_old - m_new)
                    p = jnp.exp2(s_t - m_new)
                    parts.append((m_new, alpha * l_old + jnp.sum(p, axis=0, keepdims=True), alpha, p.astype(BF16)))
                outs = [_dot(v_t, p) for _, _, _, p in parts]
                news = [(m_new, l_new, alpha * acc + o)
                        for (m_new, l_new, alpha, _), (_, _, acc), o in zip(parts, states, outs)]
            return news

        def trip(t):
            rs = [t * group + g for g in range(group)]
            states = [tuple(ref[r] for ref in state_refs) for r in rs]
            chunk = min(group, TILES_PER_STAGE[variant])
            news = []
            for c in range(0, group, chunk):
                news += staged(rs[c:c + chunk], states[c:c + chunk])
            for r, new in zip(rs, news):
                for ref, val in zip(state_refs, new):
                    ref[r] = val

        if n_tiles == group:
            trip(0)
        else:
            def body(t, carry):
                trip(t)
                return carry
            lax.fori_loop(0, n_tiles // group, body, 0)

    if sample:
        @pl.when(step == 0)
        def _new_rows():
            pad = PAGE_SIZE - tq
            k = jnp.concatenate([k_ref[...], jnp.zeros((pad, k_ref.shape[1]), F32)], axis=0)
            v = jnp.concatenate([v_ref[...], jnp.zeros((pad, v_ref.shape[1]), F32)], axis=0)
            block(k, v, past_len, True)

        ch = n_chunks - 1 - step
        v_is_t = variant == "b"
        k = jnp.concatenate([r[...] for r in kp_refs], axis=0)
        v = jnp.concatenate([r[...] for r in vp_refs], axis=1 if v_is_t else 0)
        block(k, v, ch * (pages * PAGE_SIZE), False, v_is_t)
    else:
        @pl.when(step == 0)
        def _diag():
            block(k_ref[...], v_ref[...], qi * KEY_BLOCK, True)

        @pl.when((step > 0) & (step <= qi))
        def _past():
            block(k_ref[...], v_ref[...], (qi - step) * KEY_BLOCK, False)

    @pl.when(step == last_step)
    def _fin():
        if variant == "a":
            lp = lam_ref[...]
            lam = (jnp.exp(jnp.sum(lp[0:1] * lp[1:2], axis=-1, keepdims=True))
                   - jnp.exp(jnp.sum(lp[2:3] * lp[3:4], axis=-1, keepdims=True)) + lam_init)
        n_out = M // min(rt, M)
        w = min(rt, M)
        for r in range(n_out):
            if variant == "a":
                if n_tiles == 1:
                    both = acc1[0] / l1[0]
                    o_t = both[:, :M] - lam * both[:, M:]
                else:
                    r2 = r + n_out
                    o_t = acc1[r] / l1[r] - lam * (acc1[r2] / l1[r2])
                o_t = (o_t * lax.rsqrt(jnp.mean(o_t * o_t, axis=0, keepdims=True) + EPS)
                       * gsub_ref[:, :w] * (1.0 - lam_init))
            elif variant == "d":
                o_t = acc1[r]
            else:
                o_t = acc1[r] / l1[r]
            o = o_t.T
            for u in range(w // min(tq, w)):
                rows = min(tq, w)
                h, j = divmod(r * w + u * rows, tq)
                oh = o[u * rows:(u + 1) * rows]
                if variant == "b":
                    oh = _dot(oh.astype(BF16), wuv_ref[h])
                o_ref[j:j + rows, h * LANES:(h + 1) * LANES] = oh.astype(o_ref.dtype)


def _attn_scratch(variant, M, params):
    rt, n_tiles = _attn_tiling(variant, M)
    dq = B_Q_WIDTH if variant == "b" else LANES
    dv = B_KV_LORA if variant == "b" else LANES
    scr = [pltpu.VMEM((n_tiles, dq, rt), BF16), pltpu.VMEM((n_tiles, 1, rt), F32), pltpu.VMEM((n_tiles, 1, rt), F32),
           pltpu.VMEM((n_tiles, dv, rt), F32)]
    if variant == "c":
        n_slots = next(a.shape[1] for a, kind in params if kind == "batch")
        scr.append(pltpu.VMEM((n_tiles, n_slots, rt), F32))
    return scr


def _attn_prompt(variant, q_arr, q_col, q_width, k_arr, k_col, k_width, v_arr, v_col, v_width, params, B, T,
                 lam_init=0.0):
    tq = min(KEY_BLOCK, T)
    nq = T // tq
    M = HEADS * tq

    def kv_spec(col, width):
        return pl.BlockSpec((tq, width), lambda b, qi, ki: (b * nq + jnp.maximum(qi - ki, 0), col))

    in_specs = [pl.BlockSpec((tq, q_width), lambda b, qi, ki: (b * nq + qi, q_col)),
                kv_spec(k_col, k_width), kv_spec(v_col, v_width)]
    args = [q_arr, k_arr, v_arr]
    for arr, kind in params:
        if kind == "tab":
            in_specs.append(pl.BlockSpec((tq, LANES), lambda b, qi, ki: (qi, 0)))
        elif kind == "batch":
            in_specs.append(pl.BlockSpec((None,) + arr.shape[1:], lambda b, qi, ki: (b,) + (0,) * (arr.ndim - 1)))
        else:
            in_specs.append(_full(arr.shape))
        args.append(arr)
    kern = functools.partial(_attn_kernel, variant=variant, tq=tq, sample=False, pages=0, n_chunks=nq,
                             past_len=0, lam_init=lam_init)
    return pl.pallas_call(
        kern, grid=(B, nq, nq), in_specs=in_specs,
        out_specs=pl.BlockSpec((tq, HEADS * LANES), lambda b, qi, ki: (b * nq + qi, 0)),
        out_shape=jax.ShapeDtypeStruct((B * T, HEADS * LANES), BF16),
        scratch_shapes=_attn_scratch(variant, M, params),
        compiler_params=_cparams(("parallel", "parallel", "arbitrary")), name="attn_prompt_" + variant)(*args)


def _attn_sample(variant, q_arr, q_col, q_width, k_arr, k_col, k_width, v_arr, v_col, v_width,
                 k_pool, v_pool, layer, page_table, params, pages, lam_init=0.0):
    Bs, Ts = q_arr.shape[:2]
    n_pages = page_table.shape[1]
    pages = min(pages, n_pages)
    n_chunks = n_pages // pages
    M = HEADS * Ts

    def new_spec(col, width):
        return pl.BlockSpec((None, Ts, width), lambda b, c, pt: (b, 0, col))

    def page_spec(pool, j):
        return pl.BlockSpec((None, None) + pool.shape[2:],
                            lambda b, c, pt: (layer, pt[b, (n_chunks - 1 - c) * pages + j], 0, 0))

    in_specs = [new_spec(q_col, q_width), new_spec(k_col, k_width), new_spec(v_col, v_width)]
    in_specs += [page_spec(k_pool, j) for j in range(pages)]
    in_specs += [page_spec(v_pool, j) for j in range(pages)]
    args = [q_arr, k_arr, v_arr] + [k_pool] * pages + [v_pool] * pages
    for arr, kind in params:
        if kind == "batch":
            in_specs.append(pl.BlockSpec((None,) + arr.shape[1:], lambda b, c, pt: (b,) + (0,) * (arr.ndim - 1)))
        else:
            in_specs.append(_full(arr.shape))
        args.append(arr)
    gs = pltpu.PrefetchScalarGridSpec(
        num_scalar_prefetch=1, grid=(Bs, n_chunks), in_specs=in_specs,
        out_specs=pl.BlockSpec((None, Ts, HEADS * LANES), lambda b, c, pt: (b, 0, 0)),
        scratch_shapes=_attn_scratch(variant, M, params))
    kern = functools.partial(_attn_kernel, variant=variant, tq=Ts, sample=True, pages=pages, n_chunks=n_chunks,
                             past_len=n_pages * PAGE_SIZE, lam_init=lam_init)
    return pl.pallas_call(
        kern, grid_spec=gs, out_shape=jax.ShapeDtypeStruct((Bs, Ts, HEADS * LANES), F32),
        compiler_params=_cparams(("parallel", "arbitrary")), name="attn_sample_" + variant)(page_table, *args)


def _pad_last(a, n):
    return jnp.pad(a, [(0, 0)] * (a.ndim - 1) + [(0, n - a.shape[-1])])


def kernel(x_prompt, x_sample, cache_a_k, cache_a_v, cache_b_ckv, cache_b_kr, cache_c_k, cache_c_v, cache_d_k, cache_d_v, page_table, c_prompt, c_sample, w_ada, b_ada, ada_table, w_in_even, a_q_norm, a_k_norm, a_lambda_q1, a_lambda_k1, a_lambda_q2, a_lambda_k2, a_sub_norm, b_q_lat_norm, b_w_q_up, b_q_norm, b_kv_norm, b_kr_norm, b_w_uk, b_w_uv, w_out_even, w_in_odd, c_q_norm, c_k_norm, w_out_odd, w_mlp_up, w_mlp_down):
    B, T, D = x_prompt.shape
    Bs, Ts, _ = x_sample.shape
    depth = ada_table.shape[0]
    n_pages = page_table.shape[1]
    past_len = n_pages * PAGE_SIZE
    Mp, Ms = B * T, Bs * Ts
    n_even = w_in_even.shape[0]
    n_odd = w_in_odd.shape[0]

    wie = w_in_even
    w_in_e = jnp.concatenate(
        [wie[..., 0:2048], wie[..., 2304:3328], wie[..., 2048:2304], wie[..., 3328:3648],
         jnp.zeros(wie.shape[:2] + (EVEN_WIDTH - 3648,), wie.dtype)], axis=-1).astype(BF16)
    wio = w_in_odd
    w_in_o = jnp.concatenate([wio[..., 0:2048], wio[..., 2304:4352], wio[..., 2048:2304], wio[..., 4352:4608]],
                             axis=-1).astype(BF16)
    w_qup = _pad_last(b_w_q_up, B_HEAD_SLAB).reshape(n_even, B_Q_LORA, HEADS * B_HEAD_SLAB).astype(BF16)
    w_uk_t = jnp.transpose(b_w_uk, (0, 2, 3, 1)).astype(BF16)
    w_uv = jnp.transpose(b_w_uv, (0, 2, 1, 3)).astype(BF16)
    w_out_e = w_out_even.astype(BF16)
    w_out_o = w_out_odd.astype(BF16)
    w_up = w_mlp_up
    w_down = w_mlp_down.astype(BF16)
    w_ada_b = w_ada[None]

    g_qa = jnp.tile(a_q_norm, (1, 2))[:, None, :]
    g_ka = jnp.tile(a_k_norm, (1, 2))[:, None, :]
    g_sub = jnp.broadcast_to(a_sub_norm[:, :, None], a_sub_norm.shape + (MAX_ROW_TILE,))
    g_qlb = b_q_lat_norm[:, None, :]
    g_bq = _pad_last(b_q_norm, B_HEAD_SLAB)[:, None, :]
    g_ckv = b_kv_norm[:, None, :]
    g_kr = _pad_last(b_kr_norm, LANES)[:, None, :]
    g_qc = c_q_norm[:, None, :]
    g_kc = c_k_norm[:, None, :]
    lam_rows = jnp.stack([a_lambda_q1, a_lambda_k1, a_lambda_q2, a_lambda_k2], axis=1)
    lam_rows = jnp.pad(lam_rows, ((0, 0), (0, 4), (0, LANES - A_DIM)))
    tri = jnp.asarray(np.triu(np.ones((SUFFIX_BLOCK, SUFFIX_BLOCK), np.float32), 1), BF16)
    kr_pool_t = jnp.swapaxes(cache_b_kr, 2, 3)

    pos_p = jnp.arange(T)
    pos_s = past_len + jnp.arange(Ts)
    tm_p = _pick_tile(Mp, 256)
    tm_s = _pick_tile(Ms, 256)

    def tables(pos, tile_rows):
        reps = max(1, tile_rows // pos.shape[0])
        mk = lambda rot, width, period: tuple(jnp.tile(t, (reps, 1)) for t in _rope_tables(pos, rot, width, period))
        return mk(A_ROT, LANES, A_DIM), mk(B_ROPE, B_ROPE, LANES), mk(C_ROT, LANES, LANES)

    tabs_p = tables(pos_p, tm_p)
    tabs_s = tables(pos_s, tm_s)
    tabs_s8 = tables(pos_s, Ts)

    c_all = jnp.concatenate([c_prompt, c_sample], axis=0)
    n_c = c_all.shape[0]
    c_all = jnp.pad(c_all, ((0, (-n_c) % 16), (0, 0)))
    base = _matmul(c_all, w_ada_b, 0, out_dtype=F32, tm=c_all.shape[0], tn=1024, silu=True,
                   bias=b_ada[None, :])[:n_c].reshape(n_c, ADA_CHUNKS, D)

    xp = x_prompt.reshape(Mp, D)
    xs = x_sample.reshape(Ms, D)
    new_rows = {n: ([], []) for n in ("a_k", "a_v", "b_ckv", "b_kr", "c_k", "c_v", "d_k", "d_v")}

    for l in range(depth):
        i = l // 2
        mod = base + ada_table[l]
        mod_p = mod[:B].reshape(B * ADA_CHUNKS, 1, D)
        mod_s = jnp.repeat(jnp.transpose(mod[B:], (1, 0, 2)), Ts, axis=1)

        def p_spec(chunk, width=None):
            def make(tm, tn=D):
                per_batch = T // tm
                return pl.BlockSpec((None, 1, tn), lambda r, *jk: ((r // per_batch) * ADA_CHUNKS + chunk, 0,
                                                                    jk[0] if jk else 0))
            return make

        def s_spec(chunk):
            def make(tm, tn=D):
                return pl.BlockSpec((None, tm, tn), lambda r, *jk: (chunk, r, jk[0] if jk else 0))
            return make

        groups = (("p", xp, mod_p, p_spec), ("s", xs, mod_s, s_spec))
        new_x = {}
        for name, x, md, spec in groups:
            h = _prenorm(x, md, spec(0), spec(1))
            if l % 2 == 0:
                z = _matmul(h, w_in_e, i, out_dtype=F32)
                tabs = tabs_p if name == "p" else tabs_s
                tm = tm_p if name == "p" else tm_s
                qlbn, ka, ckv, kr = _prep_even(z, tabs[0], tabs[1], g_ka[i], g_qlb[i], g_ckv[i], g_kr[i], tm)
                qh = _matmul(qlbn, w_qup, i, out_dtype=F32)
                lam_init = 0.8 - 0.6 * math.exp(-0.3 * l)
                if name == "p":
                    pa = [(t, "tab") for t in tabs_p[0]] + [(g_qa[i], "full"), (lam_rows[i], "full"), (g_sub[i], "full")]
                    o_a = _attn_prompt("a", z, 0, 2048, ka, 0, LANES, z, EVEN_VA // LANES, LANES, pa, B, T, lam_init)
                    pb = [(t, "tab") for t in tabs_p[1]] + [(g_bq[i], "full"), (w_uk_t[i], "full"), (w_uv[i], "full")]
                    o_b = _attn_prompt("b", qh, 0, HEADS * B_HEAD_SLAB, ckv, 0, B_KV_LORA, kr, 0, LANES, pb, B, T)
                    va = z[:, EVEN_VA:EVEN_VA + LANES]
                else:
                    z3 = z.reshape(Bs, Ts, EVEN_WIDTH)
                    pa = [(t, "full") for t in tabs_s8[0]] + [(g_qa[i], "full"), (lam_rows[i], "full"), (g_sub[i], "full")]
                    o_a = _attn_sample("a", z3, 0, 2048, ka.reshape(Bs, Ts, LANES), 0, LANES, z3, EVEN_VA // LANES,
                                       LANES, cache_a_k, cache_a_v, i, page_table, pa, SAMPLE_PAGES_PER_STEP, lam_init)
                    pb = [(t, "full") for t in tabs_s8[1]] + [(g_bq[i], "full"), (w_uk_t[i], "full"), (w_uv[i], "full")]
                    o_b = _attn_sample("b", qh.reshape(Bs, Ts, -1), 0, HEADS * B_HEAD_SLAB,
                                       ckv.reshape(Bs, Ts, B_KV_LORA), 0, B_KV_LORA, kr.reshape(Bs, Ts, LANES), 0,
                                       LANES, cache_b_ckv, kr_pool_t, i, page_table, pb, SAMPLE_PAGES_PER_STEP)
                    o_a = o_a.reshape(Ms, -1).astype(BF16)
                    o_b = o_b.reshape(Ms, -1).astype(BF16)
                    va = z[:, EVEN_VA:EVEN_VA + LANES]
                o, o2 = o_a, o_b
                w_out = w_out_e
                outs = (("a_k", ka), ("a_v", va), ("b_ckv", ckv), ("b_kr", kr[:, :B_ROPE]))
            else:
                z = _matmul(h, w_in_o, i, out_dtype=F32)
                tabs = tabs_p if name == "p" else tabs_s
                tm = tm_p if name == "p" else tm_s
                kc = _prep_odd(z, tabs[2], g_kc[i], tm)
                vc = z[:, ODD_VC:ODD_VC + LANES]
                kd = z[:, ODD_KD:ODD_KD + LANES]
                vd = z[:, ODD_VD:ODD_VD + LANES]
                if name == "p":
                    kmean = _kmean_prompt(kc, B, T)
                    kmean = jnp.pad(kmean, ((0, 0), (0, (-kmean.shape[1]) % 8), (0, 0)))
                    pc = [(t, "tab") for t in tabs_p[2]] + [(g_qc[i], "full"), (kmean, "batch")]
                    o_c = _attn_prompt("c", z, 0, 2048, kc, 0, LANES, z, ODD_VC // LANES, LANES, pc, B, T)
                    o_d = _attn_prompt("d", z, 1, 2048, z, ODD_KD // LANES, LANES, z, ODD_VD // LANES, LANES,
                                       [(tri, "full")], B, T)
                else:
                    z3 = z.reshape(Bs, Ts, ODD_WIDTH)
                    kmean = _kmean_sample(cache_c_k, i, page_table)
                    kmean = jnp.pad(kmean, ((0, 0), (0, (-kmean.shape[1]) % 8), (0, 0)))
                    pc = [(t, "full") for t in tabs_s8[2]] + [(g_qc[i], "full"), (kmean, "batch")]
                    o_c = _attn_sample("c", z3, 0, 2048, kc.reshape(Bs, Ts, LANES), 0, LANES, z3, ODD_VC // LANES,
                                       LANES, cache_c_k, cache_c_v, i, page_table, pc, SAMPLE_PAGES_PER_STEP)
                    o_d = _attn_sample("d", z3, 1, 2048, z3, ODD_KD // LANES, LANES, z3, ODD_VD // LANES, LANES,
                                       cache_d_k, cache_d_v, i, page_table, [(tri, "full")], SAMPLE_PAGES_PER_STEP)
                    o_c = o_c.reshape(Ms, -1).astype(BF16)
                    o_d = o_d.reshape(Ms, -1).astype(BF16)
                o, o2 = o_c, o_d
                w_out = w_out_o
                outs = (("c_k", kc), ("c_v", vc), ("d_k", kd), ("d_v", vd))
            for n, r in outs:
                new_rows[n][0 if name == "p" else 1].append(r)
            x = _matmul(o, w_out, i, out_dtype=F32, tn=512, res=x, gate=md, gate_spec=spec(2), x2=o2)
            h2 = _prenorm(x, md, spec(3), spec(4))
            u = _matmul(h2, w_up, l, out_dtype=BF16, tn=512, relu2=True)
            x = _matmul(u, w_down, l, out_dtype=F32, tk=2048, res=x, gate=md, gate_spec=spec(5))
            new_x[name] = x
        xp, xs = new_x["p"], new_x["s"]

    def stack(n):
        rp, rs = new_rows[n]
        return (jnp.stack(rp).reshape(len(rp), B, T, -1), jnp.stack(rs).reshape(len(rs), Bs, Ts, -1))

    out = [xp.reshape(B, T, D), xs.reshape(Bs, Ts, D)]
    for n in ("a_k", "a_v", "b_ckv", "b_kr", "c_k", "c_v", "d_k", "d_v"):
        out.extend(stack(n))
    return tuple(out)
```
